```python
import math
import jax
import jax.numpy as jnp
from jax import lax
import numpy as np

D_MODEL = 2048
BATCH = 8
SEQ = 4096
DEPTH = 4

CHUNK = 64
N_EVEN = (DEPTH + 1) // 2
N_ODD = DEPTH // 2
D_FF = ((8 * D_MODEL // 3 + 255) // 256) * 256
RMS_EPS = 1e-6
LN_EPS = 1e-5
NEG_INF = -1e30

CONV_W = 3
A_WIDTH = D_MODEL // 2
POOL_WINDOWS = (2, 4, 8, 16)
N_POOL = len(POOL_WINDOWS)
B_WIDTH = D_MODEL // 2
B_GROUP = B_WIDTH // N_POOL
AB_IN = 3 * A_WIDTH + B_WIDTH

C_HEADS = D_MODEL // 256
C_HEAD_DIM = 128
C_WIDTH = C_HEADS * C_HEAD_DIM
IDX_HEADS = 16
IDX_DIM = 64
IDX_W_SCALE = IDX_HEADS ** -0.5
TOPK_MAX = 256
Q_BLOCK = 128
NUM_BUCKETS = 32
MAX_DISTANCE = 128

R_HEADS = D_MODEL // 256
R_QK_DIM = 64
R_V_DIM = 128
R_WIDTH = R_HEADS * R_V_DIM
ROPE_BASE = 10000.0

CD_SIZES = (C_WIDTH, C_WIDTH, C_WIDTH,
            IDX_HEADS * IDX_DIM, IDX_DIM, IDX_HEADS,
            R_HEADS * R_QK_DIM, R_HEADS * R_QK_DIM,
            R_WIDTH, R_WIDTH)
CD_IN = sum(CD_SIZES)

kernel_name = 'hybrid_chunk_causal_conv_pool_dsa_retention'


def _split(a, sizes):
    outs, off = [], 0
    for s in sizes:
        outs.append(a[..., off:off + s])
        off += s
    return outs


def rmsnorm(x, g):
    xf = x.astype(jnp.float32)
    y = xf * lax.rsqrt(jnp.mean(xf * xf, axis=-1, keepdims=True) + RMS_EPS)
    return (y * g.astype(jnp.float32)).astype(x.dtype)


def swiglu_ffn(h, w_gate_up, w_down):
    gu = h @ w_gate_up
    gate, up = gu[..., :D_FF], gu[..., D_FF:]
    return (jax.nn.silu(gate) * up) @ w_down


def short_gated_conv(a_in, conv_w):
    b, c, h = _split(a_in, (A_WIDTH, A_WIDTH, A_WIDTH))
    u = c * h
    T = u.shape[1]
    up = jnp.pad(u, ((0, 0), (CONV_W - 1, 0), (0, 0)))
    y = sum(conv_w[j] * up[:, j:j + T] for j in range(CONV_W))
    return b * y


def multiscale_pool(b_in, pool_w, pool_scale):
    Bsz, T, _ = b_in.shape
    ug = b_in.reshape(Bsz, T, N_POOL, B_GROUP).astype(jnp.float32)
    cs = jnp.pad(jnp.cumsum(ug, axis=1), ((0, 0), (1, 0), (0, 0), (0, 0)))
    t = jnp.arange(T)
    win = jnp.array(POOL_WINDOWS, dtype=jnp.int32)
    start = jnp.maximum(t[:, None] + 1 - win[None, :], 0)
    lower = cs[:, start, jnp.arange(N_POOL)[None, :], :]
    count = (t[:, None] + 1 - start).astype(jnp.float32)
    pooled = (cs[:, 1:] - lower) / count[None, :, :, None] - ug
    mixed = jnp.einsum('btgc,gcd->btgd', pooled.astype(b_in.dtype), pool_w)
    return mixed.reshape(Bsz, T, B_WIDTH) * pool_scale


def t5_bucket(rel):
    nb = NUM_BUCKETS // 2
    max_exact = nb // 2
    ret = jnp.where(rel > 0, nb, 0)
    n = jnp.abs(rel)
    large = max_exact + (jnp.log(jnp.maximum(n, 1).astype(jnp.float32) / max_exact)
                         / math.log(MAX_DISTANCE / max_exact) * (nb - max_exact)).astype(jnp.int32)
    large = jnp.minimum(large, nb - 1)
    return ret + jnp.where(n < max_exact, n, large)


def dsa_attention(q, k, v, q_idx, k_idx, w_idx, rel_bias_table):
    Bsz, T, H, Dh = q.shape
    topk = min(TOPK_MAX, T // 4)
    n_blocks = T // Q_BLOCK
    key_pos = jnp.arange(T)
    k_idx32 = k_idx.astype(jnp.float32)
    table = rel_bias_table.astype(jnp.float32)
    gather = jax.vmap(lambda a, i: a[i])

    def one_block(blk):
        t0 = blk * Q_BLOCK
        qb = lax.dynamic_slice_in_dim(q, t0, Q_BLOCK, axis=1)
        qib = lax.dynamic_slice_in_dim(q_idx, t0, Q_BLOCK, axis=1).astype(jnp.float32)
        wib = lax.dynamic_slice_in_dim(w_idx, t0, Q_BLOCK, axis=1).astype(jnp.float32) * IDX_W_SCALE
        q_pos = t0 + jnp.arange(Q_BLOCK)
        vis_end = (q_pos // CHUNK + 1) * CHUNK
        dots = jnp.einsum('bqhd,bsd->bqhs', qib, k_idx32) * (IDX_DIM ** -0.5)
        score = jnp.einsum('bqh,bqhs->bqs', wib, jax.nn.relu(dots))
        admissible = key_pos[None, :] < vis_end[:, None]
        score = jnp.where(admissible[None], score, NEG_INF)
        _, sel = lax.top_k(score, topk)
        valid = sel < vis_end[None, :, None]
        kg = gather(k, sel)
        vg = gather(v, sel)
        logits = jnp.einsum('bqhd,bqkhd->bqhk', qb, kg).astype(jnp.float32) * (Dh ** -0.5)
        bias = table[t5_bucket(sel - q_pos[None, :, None])]
        logits = logits + jnp.moveaxis(bias, -1, 2)
        logits = jnp.where(valid[:, :, None, :], logits, NEG_INF)
        p = jax.nn.softmax(logits, axis=-1).astype(v.dtype)
        return jnp.einsum('bqhk,bqkhd->bqhd', p, vg)

    out = lax.map(one_block, jnp.arange(n_blocks))
    return jnp.moveaxis(out, 0, 1).reshape(Bsz, T, H * Dh)


def rope(x, pos):
    half = x.shape[-1] // 2
    freqs = ROPE_BASE ** (-jnp.arange(half, dtype=jnp.float32) / half)
    ang = pos.astype(jnp.float32)[:, None] * freqs[None, :]
    cos = jnp.cos(ang)[None, :, None, :]
    sin = jnp.sin(ang)[None, :, None, :]
    x1, x2 = x[..., :half], x[..., half:]
    return jnp.concatenate([x1 * cos - x2 * sin, x1 * sin + x2 * cos], axis=-1)


def retention(q, k, v, g, ret_gain):
    Bsz, T, H, dk = q.shape
    dv = v.shape[-1]
    nc = T // CHUNK
    pos = jnp.arange(T)
    qf = rope(q.astype(jnp.float32), pos)
    kf = rope(k.astype(jnp.float32), pos) * (dk ** -0.5)
    vf = v.astype(jnp.float32)
    log_g = jnp.log(1.0 - 2.0 ** (-5.0 - jnp.arange(H, dtype=jnp.float32)))
    qc = qf.reshape(Bsz, nc, CHUNK, H, dk)
    kc = kf.reshape(Bsz, nc, CHUNK, H, dk)
    vc = vf.reshape(Bsz, nc, CHUNK, H, dv)
    i = jnp.arange(CHUNK)
    intra = jnp.exp(jnp.abs(i[:, None] - i[None, :]).astype(jnp.float32)[None] * log_g[:, None, None])
    a = jnp.einsum('bnihd,bnjhd->bnhij', qc, kc) * intra[None, None]
    o_intra = jnp.einsum('bnhij,bnjhe->bnihe', a, vc)
    k_dec = jnp.exp((CHUNK - 1 - i).astype(jnp.float32)[:, None] * log_g[None, :])
    kv = jnp.einsum('bnjhd,jh,bnjhe->nbhde', kc, k_dec, vc)
    chunk_dec = jnp.exp(CHUNK * log_g)[None, :, None, None]

    def step(state, kv_n):
        return state * chunk_dec + kv_n, state

    _, prev = lax.scan(step, jnp.zeros((Bsz, H, dk, dv), jnp.float32), kv)
    q_dec = jnp.exp((i + 1).astype(jnp.float32)[:, None] * log_g[None, :])
    o_inter = jnp.einsum('bnihd,ih,nbhde->bnihe', qc, q_dec, prev)
    o = (o_intra + o_inter).reshape(Bsz, T, H, dv)
    mu = jnp.mean(o, axis=-1, keepdims=True)
    var = jnp.mean(jnp.square(o - mu), axis=-1, keepdims=True)
    o = ((o - mu) * lax.rsqrt(var + LN_EPS)).reshape(Bsz, T, H * dv) * ret_gain.astype(jnp.float32)
    return (jax.nn.silu(g.astype(jnp.float32)) * o).astype(g.dtype)


def conv_pool_mixer(u, w_in, conv_w, pool_w, pool_scale, w_out):
    proj = u @ w_in
    ya = short_gated_conv(proj[..., :3 * A_WIDTH], conv_w)
    yb = multiscale_pool(proj[..., 3 * A_WIDTH:], pool_w, pool_scale)
    return jnp.concatenate([ya, yb], axis=-1) @ w_out


def sparse_retention_mixer(u, w_in, ret_gain, rel_bias_table, w_out):
    Bsz, T, _ = u.shape
    proj = u @ w_in
    cq, ck, cv, iq, ik, iw, rq, rk, rv, rg = _split(proj, CD_SIZES)
    heads = lambda a, h: a.reshape(Bsz, T, h, -1)
    yc = dsa_attention(heads(cq, C_HEADS), heads(ck, C_HEADS), heads(cv, C_HEADS),
                       heads(iq, IDX_HEADS), ik, iw, rel_bias_table)
    yd = retention(heads(rq, R_HEADS), heads(rk, R_HEADS), heads(rv, R_HEADS), rg, ret_gain)
    return jnp.concatenate([yc, yd], axis=-1) @ w_out


def setup_inputs(seed: int = 0) -> dict:
    key = jax.random.key(seed)
    ks = jax.random.split(key, 16)
    f32 = jnp.float32
    nrm = lambda k, shape, scale: jax.random.normal(k, shape, f32) * scale
    return {
        'x': nrm(ks[0], (BATCH, SEQ, D_MODEL), 1.0),
        'norm_gains': 1.0 + nrm(ks[1], (DEPTH, 3, D_MODEL), 0.05),
        'ffn_w_gate_up': nrm(ks[2], (DEPTH, 2, D_MODEL, 2 * D_FF), D_MODEL ** -0.5),
        'ffn_w_down': nrm(ks[3], (DEPTH, 2, D_FF, D_MODEL), D_FF ** -0.5),
        'ab_w_in': nrm(ks[4], (N_EVEN, D_MODEL, AB_IN), D_MODEL ** -0.5),
        'ab_conv_w': nrm(ks[5], (N_EVEN, CONV_W, A_WIDTH), CONV_W ** -0.5),
        'ab_pool_w': nrm(ks[6], (N_EVEN, N_POOL, B_GROUP, B_GROUP), B_GROUP ** -0.5),
        'ab_pool_scale': 1.0 + nrm(ks[7], (N_EVEN, B_WIDTH), 0.1),
        'ab_w_out': nrm(ks[8], (N_EVEN, A_WIDTH + B_WIDTH, D_MODEL), (A_WIDTH + B_WIDTH) ** -0.5),
        'cd_w_in': nrm(ks[9], (N_ODD, D_MODEL, CD_IN), D_MODEL ** -0.5),
        'ret_norm_gain': 1.0 + nrm(ks[10], (N_ODD, R_WIDTH), 0.05),
        'cd_w_out': nrm(ks[11], (N_ODD, C_WIDTH + R_WIDTH, D_MODEL), (C_WIDTH + R_WIDTH) ** -0.5),
        'rel_bias_table': nrm(ks[12], (NUM_BUCKETS, C_HEADS), 0.5),
        'final_norm': 1.0 + nrm(ks[13], (D_MODEL,), 0.05),
    }


def reference(x, norm_gains, ffn_w_gate_up, ffn_w_down, ab_w_in, ab_conv_w, ab_pool_w,
              ab_pool_scale, ab_w_out, cd_w_in, ret_norm_gain, cd_w_out, rel_bias_table,
              final_norm):
    h = x
    for layer in range(DEPTH):
        j = layer // 2
        h = h + 0.5 * swiglu_ffn(rmsnorm(h, norm_gains[layer, 0]),
                                 ffn_w_gate_up[layer, 0], ffn_w_down[layer, 0])
        u = rmsnorm(h, norm_gains[layer, 1])
        if layer % 2 == 0:
            mix = conv_pool_mixer(u, ab_w_in[j], ab_conv_w[j], ab_pool_w[j],
                                  ab_pool_scale[j], ab_w_out[j])
        else:
            mix = sparse_retention_mixer(u, cd_w_in[j], ret_norm_gain[j],
                                         rel_bias_table, cd_w_out[j])
        h = h + mix
        h = h + 0.5 * swiglu_ffn(rmsnorm(h, norm_gains[layer, 2]),
                                 ffn_w_gate_up[layer, 1], ffn_w_down[layer, 1])
    return rmsnorm(h, final_norm)
```

```python
import functools
import math

import numpy as np
import jax
import jax.numpy as jnp
from jax import lax
from jax.experimental import pallas as pl
from jax.experimental.pallas import tpu as pltpu

F32 = jnp.float32
BF16 = jnp.bfloat16
I32 = jnp.int32

LANES = 128
VMEM_LIMIT_BYTES = 56 * 1024 * 1024

CHUNK = 64
RMS_EPS = 1e-6
LN_EPS = 1e-5
NEG_INF = -1e30
CONV_W = 3
POOL_WINDOWS = (2, 4, 8, 16)
POOL_HALO = 16
C_HEAD_DIM = 128
IDX_HEADS = 16
IDX_DIM = 64
TOPK_MAX = 256
NUM_BUCKETS = 32
MAX_DISTANCE = 128
R_QK_DIM = 64
R_V_DIM = 128
ROPE_BASE = 10000.0

INT_MIN = -2 ** 31


def _sortable_key_of(value):
    bits = int(np.array(value, np.float32).view(np.int32))
    return bits if bits >= 0 else bits ^ 0x7FFFFFFF


KEY_NEG_INF = _sortable_key_of(NEG_INF)


def _params(*semantics):
    return pltpu.CompilerParams(dimension_semantics=semantics, vmem_limit_bytes=VMEM_LIMIT_BYTES)


def _resident(block_shape, index_map):
    return pl.BlockSpec(block_shape, index_map, pipeline_mode=pl.Buffered(1))


def _rmsnorm_rows(x, gain):
    y = x * lax.rsqrt(jnp.mean(x * x, axis=-1, keepdims=True) + RMS_EPS)
    return y * gain


def _ffn_kernel(h_ref, g_ref, wg_ref, wu_ref, wd_ref, o_ref, xn_ref):
    f = pl.program_id(1)

    @pl.when(f == 0)
    def _():
        x = h_ref[...]
        xn_ref[...] = _rmsnorm_rows(x, g_ref[...]).astype(BF16)
        o_ref[...] = x

    xn = xn_ref[...]
    gate = jnp.dot(xn, wg_ref[...], preferred_element_type=F32)
    up = jnp.dot(xn, wu_ref[...], preferred_element_type=F32)
    act = (0.5 * (gate * jax.nn.sigmoid(gate)) * up).astype(BF16)
    o_ref[...] += jnp.dot(act, wd_ref[...], preferred_element_type=F32)


def _ffn(h, gain, w_gate_up, w_down, *, tm, tf):
    m, d = h.shape
    d_ff = w_down.shape[0]
    nf = d_ff // tf
    return pl.pallas_call(
        _ffn_kernel,
        grid=(m // tm, nf),
        in_specs=[
            pl.BlockSpec((tm, d), lambda i, f: (i, 0)),
            _resident((1, d), lambda i, f: (0, 0)),
            pl.BlockSpec((d, tf), lambda i, f: (0, f)),
            pl.BlockSpec((d, tf), lambda i, f: (0, f + nf)),
            pl.BlockSpec((tf, d), lambda i, f: (f, 0)),
        ],
        out_specs=pl.BlockSpec((tm, d), lambda i, f: (i, 0)),
        out_shape=jax.ShapeDtypeStruct((m, d), F32),
        scratch_shapes=[pltpu.VMEM((tm, d), BF16)],
        compiler_params=_params("arbitrary", "arbitrary"),
        name="ffn",
    )(h, gain.reshape(1, d), w_gate_up, w_gate_up, w_down)


def _norm_matmul_kernel(h_ref, g_ref, w_ref, o_ref):
    xn = _rmsnorm_rows(h_ref[...], g_ref[...]).astype(BF16)
    o_ref[...] = jnp.dot(xn, w_ref[...], preferred_element_type=F32).astype(o_ref.dtype)


def _norm_matmul(h, gain, w, out_dtype, *, tm, tn):
    m, d = h.shape
    n = w.shape[1]
    return pl.pallas_call(
        _norm_matmul_kernel,
        grid=(n // tn, m // tm),
        in_specs=[
            pl.BlockSpec((tm, d), lambda j, i: (i, 0)),
            _resident((1, d), lambda j, i: (0, 0)),
            pl.BlockSpec((d, tn), lambda j, i: (0, j)),
        ],
        out_specs=pl.BlockSpec((tm, tn), lambda j, i: (i, j)),
        out_shape=jax.ShapeDtypeStruct((m, n), out_dtype),
        compiler_params=_params("arbitrary", "arbitrary"),
        name="norm_matmul",
    )(h, gain.reshape(1, d), w)


def _out_proj_kernel(h_ref, y1_ref, y2_ref, w1_ref, w2_ref, o_ref):
    acc = jnp.dot(y1_ref[...], w1_ref[...], preferred_element_type=F32)
    acc += jnp.dot(y2_ref[...], w2_ref[...], preferred_element_type=F32)
    o_ref[...] = h_ref[...] + acc


def _out_proj(h, y1, y2, w, *, tm):
    m, d = h.shape
    half = y1.shape[1]
    return pl.pallas_call(
        _out_proj_kernel,
        grid=(m // tm,),
        in_specs=[
            pl.BlockSpec((tm, d), lambda i: (i, 0)),
            pl.BlockSpec((tm, half), lambda i: (i, 0)),
            pl.BlockSpec((tm, half), lambda i: (i, 0)),
            _resident((half, d), lambda i: (0, 0)),
            _resident((half, d), lambda i: (1, 0)),
        ],
        out_specs=pl.BlockSpec((tm, d), lambda i: (i, 0)),
        out_shape=jax.ShapeDtypeStruct((m, d), F32),
        compiler_params=_params("arbitrary"),
        name="out_proj",
    )(h, y1, y2, w, w)


def _final_norm_kernel(h_ref, g_ref, o_ref):
    o_ref[...] = _rmsnorm_rows(h_ref[...], g_ref[...])


def _final_norm(h, gain, *, tm):
    m, d = h.shape
    return pl.pallas_call(
        _final_norm_kernel,
        grid=(m // tm,),
        in_specs=[pl.BlockSpec((tm, d), lambda i: (i, 0)), _resident((1, d), lambda i: (0, 0))],
        out_specs=pl.BlockSpec((tm, d), lambda i: (i, 0)),
        out_shape=jax.ShapeDtypeStruct((m, d), F32),
        compiler_params=_params("arbitrary"),
        name="final_norm",
    )(h, gain.reshape(1, d))


def _rows_with_halo(load, r0, rows, width):
    if r0 == 0:
        return jnp.concatenate([jnp.zeros((POOL_HALO, width), F32), load(0, rows)], axis=0)
    return load(r0 - POOL_HALO, rows + POOL_HALO)


def _conv_kernel(b_ref, c_ref, v_ref, w_ref, o_ref, *, rows):
    t, cb = b_ref.shape[1], b_ref.shape[2]
    w0, w1, w2 = w_ref[0:1, :], w_ref[1:2, :], w_ref[2:3, :]
    for r0 in range(0, t, rows):
        ext = _rows_with_halo(lambda s, n: c_ref[0, s:s + n, :] * v_ref[0, s:s + n, :], r0, rows, cb)
        u0 = ext[POOL_HALO:]
        u1 = pltpu.roll(ext, 1, axis=0)[POOL_HALO:]
        u2 = pltpu.roll(ext, 2, axis=0)[POOL_HALO:]
        y = w0 * u2 + w1 * u1 + w2 * u0
        o_ref[0, r0:r0 + rows, :] = (b_ref[0, r0:r0 + rows, :] * y).astype(o_ref.dtype)


def _gated_conv(proj, conv_w, *, cb, rows):
    bsz, t, _ = proj.shape
    a_width = conv_w.shape[1]
    ncb = a_width // cb
    return pl.pallas_call(
        functools.partial(_conv_kernel, rows=rows),
        grid=(bsz, ncb),
        in_specs=[
            pl.BlockSpec((1, t, cb), lambda b, c: (b, 0, c)),
            pl.BlockSpec((1, t, cb), lambda b, c: (b, 0, c + ncb)),
            pl.BlockSpec((1, t, cb), lambda b, c: (b, 0, c + 2 * ncb)),
            pl.BlockSpec((CONV_W, cb), lambda b, c: (0, c)),
        ],
        out_specs=pl.BlockSpec((1, t, cb), lambda b, c: (b, 0, c)),
        out_shape=jax.ShapeDtypeStruct((bsz, t, a_width), BF16),
        compiler_params=_params("arbitrary", "arbitrary"),
        name="gated_conv",
    )(proj, proj, proj, conv_w)


def _pool_kernel(x_ref, w_ref, s_ref, o_ref, *, rows):
    t, gw = x_ref.shape[1], x_ref.shape[2]
    group = pl.program_id(1)
    wmat = w_ref[0]
    scale = s_ref[...]
    for gi, window in enumerate(POOL_WINDOWS):
        @pl.when(group == gi)
        def _(window=window):
            for r0 in range(0, t, rows):
                ext = _rows_with_halo(lambda s, n: x_ref[0, s:s + n, :], r0, rows, gw)
                acc = ext
                span = 1
                while span < window:
                    acc = acc + pltpu.roll(acc, span, axis=0)
                    span *= 2
                pos = r0 + lax.broadcasted_iota(I32, (rows, 1), 0)
                count = jnp.minimum(pos + 1, window).astype(F32)
                pooled = acc[POOL_HALO:] / count - ext[POOL_HALO:]
                mixed = jnp.dot(pooled.astype(BF16), wmat, preferred_element_type=F32)
                o_ref[0, r0:r0 + rows, :] = (mixed * scale).astype(o_ref.dtype)


def _multiscale_pool(proj, pool_w, pool_scale, *, col0, rows):
    bsz, t, _ = proj.shape
    ngroup, gw, _ = pool_w.shape
    cblk0 = col0 // gw
    return pl.pallas_call(
        functools.partial(_pool_kernel, rows=rows),
        grid=(bsz, ngroup),
        in_specs=[
            pl.BlockSpec((1, t, gw), lambda b, g: (b, 0, cblk0 + g)),
            pl.BlockSpec((1, gw, gw), lambda b, g: (g, 0, 0)),
            pl.BlockSpec((1, gw), lambda b, g: (0, g)),
        ],
        out_specs=pl.BlockSpec((1, t, gw), lambda b, g: (b, 0, g)),
        out_shape=jax.ShapeDtypeStruct((bsz, t, ngroup * gw), BF16),
        compiler_params=_params("arbitrary", "arbitrary"),
        name="multiscale_pool",
    )(proj, pool_w, pool_scale.reshape(1, ngroup * gw))


def _row_count(acc):
    return jnp.sum(acc, axis=1, keepdims=True)


def _dsa_kernel(q_ref, k_ref, v_ref, iq_ref, ikw_ref, iwq_ref, bias_ref, o_ref,
                kk_ref, aq_ref, wb_ref, keys_ref, jcut_ref, m_ref, l_ref, acc_ref,
                *, seq, tq, topk, n_heads):
    qt = pl.program_id(1)
    kt_w = tq
    ncol = kt_w // LANES
    n_tiles = qt + 1
    n_unseen = seq - n_tiles * kt_w
    lane = lax.broadcasted_iota(I32, (tq, LANES), 1)

    @pl.when(qt == 0)
    def _():
        x = ikw_ref[0]
        lane_t = lax.broadcasted_iota(I32, x.shape, 1)
        kk_ref[...] = jnp.where(lane_t < IDX_DIM, x, pltpu.roll(x, IDX_DIM, axis=1)).astype(BF16)

    for pair in range(IDX_HEADS // 2):
        a = iq_ref[0, :, pair * LANES:(pair + 1) * LANES].astype(F32)
        aq_ref[2 * pair] = jnp.where(lane < IDX_DIM, a, 0.0).astype(BF16)
        aq_ref[2 * pair + 1] = jnp.where(lane >= IDX_DIM, a, 0.0).astype(BF16)
    w_all = iwq_ref[0]
    w_scale = (IDX_HEADS ** -0.5) * (IDX_DIM ** -0.5)
    for hd in range(IDX_HEADS):
        col = IDX_DIM + hd
        wb_ref[hd] = jnp.broadcast_to(w_all[:, col:col + 1] * w_scale, (tq, LANES))

    row = lax.broadcasted_iota(I32, (tq, kt_w), 0)
    colk = lax.broadcasted_iota(I32, (tq, kt_w), 1)
    vis_end = ((qt * tq + row) // CHUNK + 1) * CHUNK

    def tile_start(j):
        return pl.multiple_of(j * kt_w, kt_w)

    def score_tile(j, carry):
        s0 = tile_start(j)
        kk = kk_ref[pl.ds(s0, kt_w), :]
        score = jnp.zeros((tq, kt_w), F32)
        for hd in range(IDX_HEADS):
            d = lax.dot_general(aq_ref[hd], kk, (((1,), (1,)), ((), ())), preferred_element_type=F32)
            wfull = jnp.concatenate([wb_ref[hd]] * ncol, axis=1)
            score = score + wfull * jnp.maximum(d, 0.0)
        bits = pltpu.bitcast(score, I32)
        key = jnp.where(bits >= 0, bits, bits ^ 0x7FFFFFFF)
        key = jnp.where(s0 + colk < vis_end, key, KEY_NEG_INF)
        keys_ref[:, pl.ds(s0, kt_w)] = key
        return carry

    lax.fori_loop(0, n_tiles, score_tile, 0)

    def count_tiles(pred):
        def body(j, acc):
            s0 = tile_start(j)
            kt = keys_ref[:, pl.ds(s0, kt_w)]
            for c in range(ncol):
                blk = kt[:, c * LANES:(c + 1) * LANES]
                acc = acc + jnp.where(pred(blk, s0 + c * LANES + lane), 1.0, 0.0)
            return acc
        return _row_count(lax.fori_loop(0, n_tiles, body, jnp.zeros((tq, LANES), F32)))

    def unseen_if(cond):
        return jnp.where(cond, float(1.0) * n_unseen, 0.0)

    def bit_step(i, thr):
        cand = thr + lax.shift_left(jnp.int32(1), 31 - i)
        candb = jnp.broadcast_to(cand, (tq, LANES))
        cnt = count_tiles(lambda blk, idx: blk >= candb) + unseen_if(cand <= KEY_NEG_INF)
        return jnp.where(cnt >= topk, cand, thr)

    thr = lax.fori_loop(0, 32, bit_step, jnp.full((tq, 1), INT_MIN, I32))
    thrb = jnp.broadcast_to(thr, (tq, LANES))

    cnt_ge = count_tiles(lambda blk, idx: blk >= thrb) + unseen_if(thr <= KEY_NEG_INF)
    cnt_gt = count_tiles(lambda blk, idx: blk > thrb) + unseen_if(thr < KEY_NEG_INF)
    need = topk - cnt_gt
    idx_bits = int(seq).bit_length()
    jcut_ref[...] = jnp.full((tq, LANES), 2 ** idx_bits - 1, I32)
    has_tie = jnp.max(jnp.where(cnt_ge != topk, 1.0, 0.0)) > 0.0

    @pl.when(has_tie)
    def _():
        def idx_step(i, cut):
            cand = cut + lax.shift_left(jnp.int32(1), idx_bits - 1 - i)
            candb = jnp.broadcast_to(cand, (tq, LANES))
            below = count_tiles(lambda blk, idx: (blk == thrb) & (idx < candb))
            return jnp.where(below <= need, cand, cut)
        cut = lax.fori_loop(0, idx_bits, idx_step, jnp.zeros((tq, 1), I32))
        jcut_ref[...] = jnp.broadcast_to(cut, (tq, LANES))

    thr_full = jnp.concatenate([thrb] * ncol, axis=1)
    cut_full = jnp.concatenate([jcut_ref[...]] * ncol, axis=1)

    m_ref[...] = jnp.full(m_ref.shape, NEG_INF, F32)
    l_ref[...] = jnp.zeros(l_ref.shape, F32)
    acc_ref[...] = jnp.zeros(acc_ref.shape, F32)
    scale = C_HEAD_DIM ** -0.5

    def attend_tile(j, carry):
        s0 = tile_start(j)
        kt = keys_ref[:, pl.ds(s0, kt_w)]
        s_idx = s0 + colk
        sel = ((kt > thr_full) | ((kt == thr_full) & (s_idx < cut_full))) & (s_idx < vis_end)
        x0 = pl.multiple_of(jnp.clip(j - qt + 2, 0, 2) * kt_w, kt_w)
        for hd in range(n_heads):
            hs = slice(hd * C_HEAD_DIM, (hd + 1) * C_HEAD_DIM)
            logits = lax.dot_general(q_ref[0, :, hs], k_ref[0, pl.ds(s0, kt_w), hs],
                                     (((1,), (1,)), ((), ())), preferred_element_type=F32)
            logits = logits * scale + bias_ref[hd, :, pl.ds(x0, kt_w)]
            logits = jnp.where(sel, logits, NEG_INF)
            m_old = m_ref[hd]
            m_new = jnp.maximum(m_old, jnp.max(logits, axis=1, keepdims=True))
            p = jnp.exp(logits - jnp.concatenate([m_new] * ncol, axis=1))
            alpha = jnp.exp(m_old - m_new)
            l_ref[hd] = alpha * l_ref[hd] + jnp.sum(p, axis=1, keepdims=True)
            pv = jnp.dot(p.astype(BF16), v_ref[0, pl.ds(s0, kt_w), hs], preferred_element_type=F32)
            acc_ref[hd] = alpha * acc_ref[hd] + pv
            m_ref[hd] = m_new
        return carry

    lax.fori_loop(0, n_tiles, attend_tile, 0)
    for hd in range(n_heads):
        hs = slice(hd * C_HEAD_DIM, (hd + 1) * C_HEAD_DIM)
        o_ref[0, :, hs] = (acc_ref[hd] / l_ref[hd]).astype(o_ref.dtype)


def _t5_bucket(rel):
    nb = NUM_BUCKETS // 2
    max_exact = nb // 2
    ret = jnp.where(rel > 0, nb, 0)
    n = jnp.abs(rel)
    large = max_exact + (jnp.log(jnp.maximum(n, 1).astype(F32) / max_exact)
                         / math.log(MAX_DISTANCE / max_exact) * (nb - max_exact)).astype(I32)
    large = jnp.minimum(large, nb - 1)
    return ret + jnp.where(n < max_exact, n, large)


def _bias_slab(rel_bias_table, tq):
    assert tq >= MAX_DISTANCE
    rel = jnp.arange(3 * tq, dtype=I32)[None, :] - 2 * tq - jnp.arange(tq, dtype=I32)[:, None]
    return jnp.moveaxis(rel_bias_table.astype(F32)[_t5_bucket(rel)], -1, 0)


def _dsa_attention(proj_qkv, proj_rest, rel_bias_table, *, c_width, rest_idx_block, tq):
    bsz, seq, _ = proj_qkv.shape
    n_heads = c_width // C_HEAD_DIM
    topk = min(TOPK_MAX, seq // 4)
    slab = _bias_slab(rel_bias_table, tq)
    kernel = functools.partial(_dsa_kernel, seq=seq, tq=tq, topk=topk, n_heads=n_heads)
    return pl.pallas_call(
        kernel,
        grid=(bsz, seq // tq),
        in_specs=[
            pl.BlockSpec((1, tq, c_width), lambda b, t: (b, t, 0)),
            pl.BlockSpec((1, seq, c_width), lambda b, t: (b, 0, 1), pipeline_mode=pl.Buffered(1)),
            pl.BlockSpec((1, seq, c_width), lambda b, t: (b, 0, 2), pipeline_mode=pl.Buffered(1)),
            pl.BlockSpec((1, tq, IDX_HEADS * IDX_DIM), lambda b, t: (b, t, 3)),
            pl.BlockSpec((1, seq, LANES), lambda b, t: (b, 0, rest_idx_block), pipeline_mode=pl.Buffered(1)),
            pl.BlockSpec((1, tq, LANES), lambda b, t: (b, t, rest_idx_block)),
            _resident((n_heads, tq, 3 * tq), lambda b, t: (0, 0, 0)),
        ],
        out_specs=pl.BlockSpec((1, tq, c_width), lambda b, t: (b, t, 0)),
        out_shape=jax.ShapeDtypeStruct((bsz, seq, c_width), BF16),
        scratch_shapes=[
            pltpu.VMEM((seq, LANES), BF16),
            pltpu.VMEM((IDX_HEADS, tq, LANES), BF16),
            pltpu.VMEM((IDX_HEADS, tq, LANES), F32),
            pltpu.VMEM((tq, seq), I32),
            pltpu.VMEM((tq, LANES), I32),
            pltpu.VMEM((n_heads, tq, LANES), F32),
            pltpu.VMEM((n_heads, tq, LANES), F32),
            pltpu.VMEM((n_heads, tq, C_HEAD_DIM), F32),
        ],
        compiler_params=_params("arbitrary", "arbitrary"),
        name="dsa_attention",
    )(proj_qkv, proj_qkv, proj_qkv, proj_qkv, proj_rest, proj_rest, slab)


def _rope_pairs(x, cos, sin_signed):
    lane = lax.broadcasted_iota(I32, x.shape, 1)
    half = R_QK_DIM // 2
    swapped = jnp.where((lane % R_QK_DIM) < half,
                        pltpu.roll(x, LANES - half, axis=1), pltpu.roll(x, half, axis=1))
    return x * cos + swapped * sin_signed


def _retention_kernel(q_ref, k_ref, v_ref, g_ref, cos_ref, sin_ref, gain_ref, o_ref,
                      state_ref, *, blk, n_heads):
    tb = pl.program_id(1)

    @pl.when(tb == 0)
    def _():
        state_ref[...] = jnp.zeros(state_ref.shape, F32)

    lane = lax.broadcasted_iota(I32, (blk, LANES), 1)
    ri = lax.broadcasted_iota(I32, (blk, blk), 0)
    ci = lax.broadcasted_iota(I32, (blk, blk), 1)
    visible = (ci // CHUNK) <= (ri // CHUNK)
    dist = jnp.abs(ri - ci).astype(F32)
    pos = lax.broadcasted_iota(I32, (blk, 1), 0).astype(F32)

    for pair in range(n_heads // 2):
        ls = slice(pair * LANES, (pair + 1) * LANES)
        qr = _rope_pairs(q_ref[0, :, ls], cos_ref[:, ls], sin_ref[:, ls])
        kr = _rope_pairs(k_ref[0, :, ls], cos_ref[:, ls], sin_ref[:, ls]) * (R_QK_DIM ** -0.5)
        kr_b = kr.astype(BF16)
        for e in range(2):
            hd = 2 * pair + e
            vs = slice(hd * R_V_DIM, (hd + 1) * R_V_DIM)
            log_g = jnp.log(jnp.full((1, 1), 1.0 - 2.0 ** (-5.0 - hd), F32))
            mine = (lane >= e * R_QK_DIM) & (lane < (e + 1) * R_QK_DIM)
            qm = jnp.where(mine, qr, 0.0)
            km = jnp.where(mine, kr, 0.0)
            v_b = v_ref[0, :, vs].astype(BF16)
            decay = jnp.where(visible, jnp.exp(dist * log_g), 0.0)
            a = lax.dot_general(qm.astype(BF16), kr_b, (((1,), (1,)), ((), ())),
                                preferred_element_type=F32) * decay
            o = jnp.dot(a.astype(BF16), v_b, preferred_element_type=F32)
            state = state_ref[hd]
            q_dec = qm * jnp.exp((pos + 1.0) * log_g)
            o = o + jnp.dot(q_dec.astype(BF16), state.astype(BF16), preferred_element_type=F32)
            k_dec = km * jnp.exp((blk - 1.0 - pos) * log_g)
            kv = lax.dot_general(k_dec.astype(BF16), v_b, (((0,), (0,)), ((), ())),
                                 preferred_element_type=F32)
            state_ref[hd] = state * jnp.exp(blk * log_g) + kv
            mu = jnp.mean(o, axis=-1, keepdims=True)
            var = jnp.mean(jnp.square(o - mu), axis=-1, keepdims=True)
            on = (o - mu) * lax.rsqrt(var + LN_EPS) * gain_ref[:, vs]
            gate = g_ref[0, :, vs]
            o_ref[0, :, vs] = (gate * jax.nn.sigmoid(gate) * on).astype(o_ref.dtype)


def _rope_tables(seq, n_heads):
    half = R_QK_DIM // 2
    freqs = ROPE_BASE ** (-jnp.arange(half, dtype=F32) / half)
    ang = jnp.arange(seq).astype(F32)[:, None] * freqs[None, :]
    cos, sin = jnp.cos(ang), jnp.sin(ang)
    cos_t = jnp.tile(jnp.concatenate([cos, cos], axis=1), (1, n_heads))
    sin_t = jnp.tile(jnp.concatenate([-sin, sin], axis=1), (1, n_heads))
    return cos_t, sin_t


def _retention(proj_rest, ret_gain, *, r_width, blk):
    bsz, seq, _ = proj_rest.shape
    n_heads = r_width // R_V_DIM
    qk_width = n_heads * R_QK_DIM
    cos_t, sin_t = _rope_tables(seq, n_heads)
    kernel = functools.partial(_retention_kernel, blk=blk, n_heads=n_heads)
    return pl.pallas_call(
        kernel,
        grid=(bsz, seq // blk),
        in_specs=[
            pl.BlockSpec((1, blk, qk_width), lambda b, t: (b, t, 0)),
            pl.BlockSpec((1, blk, qk_width), lambda b, t: (b, t, 1)),
            pl.BlockSpec((1, blk, r_width), lambda b, t: (b, t, 1)),
            pl.BlockSpec((1, blk, r_width), lambda b, t: (b, t, 2)),
            pl.BlockSpec((blk, qk_width), lambda b, t: (t, 0)),
            pl.BlockSpec((blk, qk_width), lambda b, t: (t, 0)),
            _resident((1, r_width), lambda b, t: (0, 0)),
        ],
        out_specs=pl.BlockSpec((1, blk, r_width), lambda b, t: (b, t, 0)),
        out_shape=jax.ShapeDtypeStruct((bsz, seq, r_width), BF16),
        scratch_shapes=[pltpu.VMEM((n_heads, LANES, R_V_DIM), F32)],
        compiler_params=_params("arbitrary", "arbitrary"),
        name="retention",
    )(proj_rest, proj_rest, proj_rest, proj_rest, cos_t, sin_t, ret_gain.reshape(1, r_width))


def _largest_tile(n, cap, multiple=LANES):
    best = None
    for t in range(multiple, min(n, cap) + 1, multiple):
        if n % t == 0:
            best = t
    assert best is not None, (n, cap)
    return best


def kernel(x, norm_gains, ffn_w_gate_up, ffn_w_down, ab_w_in, ab_conv_w, ab_pool_w, ab_pool_scale,
           ab_w_out, cd_w_in, ret_norm_gain, cd_w_out, rel_bias_table, final_norm):
    bsz, seq, d = x.shape
    m = bsz * seq
    depth = norm_gains.shape[0]
    a_width = ab_conv_w.shape[-1]
    b_width = ab_pool_scale.shape[-1]
    c_width = cd_w_out.shape[1] // 2
    r_width = ret_norm_gain.shape[-1]
    qk_width = (r_width // R_V_DIM) * R_QK_DIM
    assert c_width == IDX_HEADS * IDX_DIM == r_width and 2 * qk_width == r_width

    tm = _largest_tile(m, 512, 8)
    tf = _largest_tile(ffn_w_down.shape[2], 512)
    seq_tile = _largest_tile(seq, 256, CHUNK)
    pool_rows = _largest_tile(seq, 512, 8)

    w_gu = ffn_w_gate_up.astype(BF16)
    w_dn = ffn_w_down.astype(BF16)

    h = x.reshape(m, d)
    for layer in range(depth):
        j = layer // 2
        h = _ffn(h, norm_gains[layer, 0], w_gu[layer, 0], w_dn[layer, 0], tm=tm, tf=tf)
        if layer % 2 == 0:
            w_in = ab_w_in[j].astype(BF16)
            proj = _norm_matmul(h, norm_gains[layer, 1], w_in, F32,
                                tm=tm, tn=_largest_tile(w_in.shape[1], 2048))
            proj = proj.reshape(bsz, seq, -1)
            y1 = _gated_conv(proj, ab_conv_w[j], cb=2 * LANES, rows=pool_rows)
            y2 = _multiscale_pool(proj, ab_pool_w[j].astype(BF16), ab_pool_scale[j],
                                  col0=3 * a_width, rows=pool_rows)
            w_out = ab_w_out[j].astype(BF16)
        else:
            w_in = cd_w_in[j]
            o_iq = 3 * c_width
            o_ik = o_iq + IDX_HEADS * IDX_DIM
            o_rq = o_ik + IDX_DIM + IDX_HEADS
            n_rest = 2 * qk_width + 2 * r_width
            w_qkv = w_in[:, :o_ik].astype(BF16)
            pad = LANES - (IDX_DIM + IDX_HEADS)
            w_rest = jnp.concatenate(
                [w_in[:, o_rq:o_rq + n_rest], w_in[:, o_ik:o_rq], jnp.zeros((d, pad), F32)],
                axis=1).astype(BF16)
            proj_qkv = _norm_matmul(h, norm_gains[layer, 1], w_qkv, BF16,
                                    tm=tm, tn=_largest_tile(w_qkv.shape[1], 2048))
            proj_rest = _norm_matmul(h, norm_gains[layer, 1], w_rest, F32,
                                     tm=tm, tn=_largest_tile(w_rest.shape[1], 3200))
            y1 = _dsa_attention(proj_qkv.reshape(bsz, seq, -1), proj_rest.reshape(bsz, seq, -1),
                                rel_bias_table, c_width=c_width,
                                rest_idx_block=n_rest // LANES, tq=seq_tile)
            y2 = _retention(proj_rest.reshape(bsz, seq, -1), ret_norm_gain[j],
                            r_width=r_width, blk=seq_tile)
            w_out = cd_w_out[j].astype(BF16)
        h = _out_proj(h, y1.reshape(m, -1), y2.reshape(m, -1), w_out, tm=tm)
        h = _ffn(h, norm_gains[layer, 2], w_gu[layer, 1], w_dn[layer, 1], tm=tm, tf=tf)
    return _final_norm(h, final_norm, tm=tm).reshape(bsz, seq, d)
```

```python
import functools
import math

import numpy as np
import jax
import jax.numpy as jnp
from jax import lax
from jax.experimental import pallas as pl
from jax.experimental.pallas import tpu as pltpu

F32 = jnp.float32
BF16 = jnp.bfloat16
I32 = jnp.int32

LANES = 128
VMEM_LIMIT_BYTES = 56 * 1024 * 1024

CHUNK = 64
RMS_EPS = 1e-6
LN_EPS = 1e-5
NEG_INF = -1e30
CONV_W = 3
POOL_WINDOWS = (2, 4, 8, 16)
POOL_HALO = 16
C_HEAD_DIM = 128
IDX_HEADS = 16
IDX_DIM = 64
TOPK_MAX = 256
NUM_BUCKETS = 32
MAX_DISTANCE = 128
R_QK_DIM = 64
R_V_DIM = 128
ROPE_BASE = 10000.0

INT_MIN = -2 ** 31


def _sortable_key_of(value):
    bits = int(np.array(value, np.float32).view(np.int32))
    return bits if bits >= 0 else bits ^ 0x7FFFFFFF


KEY_NEG_INF = _sortable_key_of(NEG_INF)


def _params(*semantics):
    return pltpu.CompilerParams(dimension_semantics=semantics, vmem_limit_bytes=VMEM_LIMIT_BYTES)


def _resident(block_shape, index_map):
    return pl.BlockSpec(block_shape, index_map, pipeline_mode=pl.Buffered(1))


def _rmsnorm_rows(x, gain):
    y = x * lax.rsqrt(jnp.mean(x * x, axis=-1, keepdims=True) + RMS_EPS)
    return y * gain


def _ffn_kernel(h_ref, g_ref, wg_ref, wu_ref, wd_ref, *rest, n_hidden_tiles, closing_norm):
    if closing_norm:
        fg_ref, o_ref, xn_ref = rest
    else:
        o_ref, xn_ref = rest
    f = pl.program_id(1)

    @pl.when(f == 0)
    def _():
        x = h_ref[...]
        xn_ref[...] = _rmsnorm_rows(x, g_ref[...]).astype(BF16)
        o_ref[...] = x

    xn = xn_ref[...]
    gate = jnp.dot(xn, wg_ref[...], preferred_element_type=F32)
    up = jnp.dot(xn, wu_ref[...], preferred_element_type=F32)
    act = (0.5 * (gate * jax.nn.sigmoid(gate)) * up).astype(BF16)
    o_ref[...] += jnp.dot(act, wd_ref[...], preferred_element_type=F32)

    if closing_norm:
        @pl.when(f == n_hidden_tiles - 1)
        def _():
            o_ref[...] = _rmsnorm_rows(o_ref[...], fg_ref[...])


def _ffn(h, gain, w_gate_up, w_down, layer, which, *, tm, tf, closing_gain=None):
    m, d = h.shape
    d_ff = w_down.shape[2]
    nf = d_ff // tf
    closing_norm = closing_gain is not None
    in_specs = [
        pl.BlockSpec((tm, d), lambda i, f: (i, 0)),
        _resident((1, d), lambda i, f: (0, 0)),
        pl.BlockSpec((None, None, d, tf), lambda i, f: (layer, which, 0, f)),
        pl.BlockSpec((None, None, d, tf), lambda i, f: (layer, which, 0, f + nf)),
        pl.BlockSpec((None, None, tf, d), lambda i, f: (layer, which, f, 0)),
    ]
    args = [h, gain.reshape(1, d), w_gate_up, w_gate_up, w_down]
    if closing_norm:
        in_specs.append(_resident((1, d), lambda i, f: (0, 0)))
        args.append(closing_gain.reshape(1, d))
    return pl.pallas_call(
        functools.partial(_ffn_kernel, n_hidden_tiles=nf, closing_norm=closing_norm),
        grid=(m // tm, nf),
        in_specs=in_specs,
        out_specs=pl.BlockSpec((tm, d), lambda i, f: (i, 0)),
        out_shape=jax.ShapeDtypeStruct((m, d), F32),
        scratch_shapes=[pltpu.VMEM((tm, d), BF16)],
        compiler_params=_params("arbitrary", "arbitrary"),
        name="ffn",
    )(*args)


def _norm_matmul_kernel(h_ref, g_ref, w_ref, o_ref):
    xn = _rmsnorm_rows(h_ref[...], g_ref[...]).astype(BF16)
    o_ref[...] = jnp.dot(xn, w_ref[...], preferred_element_type=F32).astype(o_ref.dtype)


def _norm_matmul(h, gain, w, out_dtype, *, tm, tn):
    m, d = h.shape
    n = w.shape[1]
    return pl.pallas_call(
        _norm_matmul_kernel,
        grid=(n // tn, m // tm),
        in_specs=[
            pl.BlockSpec((tm, d), lambda j, i: (i, 0)),
            _resident((1, d), lambda j, i: (0, 0)),
            pl.BlockSpec((d, tn), lambda j, i: (0, j)),
        ],
        out_specs=pl.BlockSpec((tm, tn), lambda j, i: (i, j)),
        out_shape=jax.ShapeDtypeStruct((m, n), out_dtype),
        compiler_params=_params("arbitrary", "arbitrary"),
        name="norm_matmul",
    )(h, gain.reshape(1, d), w)


def _out_proj_kernel(h_ref, y1_ref, y2_ref, w1_ref, w2_ref, o_ref):
    acc = jnp.dot(y1_ref[...], w1_ref[...], preferred_element_type=F32)
    acc += jnp.dot(y2_ref[...], w2_ref[...], preferred_element_type=F32)
    o_ref[...] = h_ref[...] + acc


def _out_proj(h, y1, y2, w, *, tm):
    m, d = h.shape
    half = y1.shape[1]
    return pl.pallas_call(
        _out_proj_kernel,
        grid=(m // tm,),
        in_specs=[
            pl.BlockSpec((tm, d), lambda i: (i, 0)),
            pl.BlockSpec((tm, half), lambda i: (i, 0)),
            pl.BlockSpec((tm, half), lambda i: (i, 0)),
            _resident((half, d), lambda i: (0, 0)),
            _resident((half, d), lambda i: (1, 0)),
        ],
        out_specs=pl.BlockSpec((tm, d), lambda i: (i, 0)),
        out_shape=jax.ShapeDtypeStruct((m, d), F32),
        compiler_params=_params("arbitrary"),
        name="out_proj",
    )(h, y1, y2, w, w)


def _rows_with_halo(load, r0, rows, width):
    if r0 == 0:
        return jnp.concatenate([jnp.zeros((POOL_HALO, width), F32), load(0, rows)], axis=0)
    return load(r0 - POOL_HALO, rows + POOL_HALO)


def _conv_kernel(b_ref, c_ref, v_ref, w_ref, o_ref, *, rows):
    t, cb = b_ref.shape[1], b_ref.shape[2]
    w0, w1, w2 = w_ref[0:1, :], w_ref[1:2, :], w_ref[2:3, :]
    for r0 in range(0, t, rows):
        ext = _rows_with_halo(lambda s, n: c_ref[0, s:s + n, :] * v_ref[0, s:s + n, :], r0, rows, cb)
        u0 = ext[POOL_HALO:]
        u1 = pltpu.roll(ext, 1, axis=0)[POOL_HALO:]
        u2 = pltpu.roll(ext, 2, axis=0)[POOL_HALO:]
        y = w0 * u2 + w1 * u1 + w2 * u0
        o_ref[0, r0:r0 + rows, :] = (b_ref[0, r0:r0 + rows, :] * y).astype(o_ref.dtype)


def _gated_conv(proj, conv_w, *, cb, rows):
    bsz, t, _ = proj.shape
    a_width = conv_w.shape[1]
    ncb = a_width // cb
    return pl.pallas_call(
        functools.partial(_conv_kernel, rows=rows),
        grid=(bsz, ncb),
        in_specs=[
            pl.BlockSpec((1, t, cb), lambda b, c: (b, 0, c)),
            pl.BlockSpec((1, t, cb), lambda b, c: (b, 0, c + ncb)),
            pl.BlockSpec((1, t, cb), lambda b, c: (b, 0, c + 2 * ncb)),
            pl.BlockSpec((CONV_W, cb), lambda b, c: (0, c)),
        ],
        out_specs=pl.BlockSpec((1, t, cb), lambda b, c: (b, 0, c)),
        out_shape=jax.ShapeDtypeStruct((bsz, t, a_width), BF16),
        compiler_params=_params("arbitrary", "arbitrary"),
        name="gated_conv",
    )(proj, proj, proj, conv_w)


def _pool_kernel(x_ref, w_ref, s_ref, o_ref, *, rows):
    t, gw = x_ref.shape[1], x_ref.shape[2]
    group = pl.program_id(1)
    wmat = w_ref[0]
    scale = s_ref[...]
    for gi, window in enumerate(POOL_WINDOWS):
        @pl.when(group == gi)
        def _(window=window):
            for r0 in range(0, t, rows):
                ext = _rows_with_halo(lambda s, n: x_ref[0, s:s + n, :], r0, rows, gw)
                acc = ext
                span = 1
                while span < window:
                    acc = acc + pltpu.roll(acc, span, axis=0)
                    span *= 2
                pos = r0 + lax.broadcasted_iota(I32, (rows, 1), 0)
                count = jnp.minimum(pos + 1, window).astype(F32)
                pooled = acc[POOL_HALO:] / count - ext[POOL_HALO:]
                mixed = jnp.dot(pooled.astype(BF16), wmat, preferred_element_type=F32)
                o_ref[0, r0:r0 + rows, :] = (mixed * scale).astype(o_ref.dtype)


def _multiscale_pool(proj, pool_w, pool_scale, *, col0, rows):
    bsz, t, _ = proj.shape
    ngroup, gw, _ = pool_w.shape
    cblk0 = col0 // gw
    return pl.pallas_call(
        functools.partial(_pool_kernel, rows=rows),
        grid=(bsz, ngroup),
        in_specs=[
            pl.BlockSpec((1, t, gw), lambda b, g: (b, 0, cblk0 + g)),
            pl.BlockSpec((1, gw, gw), lambda b, g: (g, 0, 0)),
            pl.BlockSpec((1, gw), lambda b, g: (0, g)),
        ],
        out_specs=pl.BlockSpec((1, t, gw), lambda b, g: (b, 0, g)),
        out_shape=jax.ShapeDtypeStruct((bsz, t, ngroup * gw), BF16),
        compiler_params=_params("arbitrary", "arbitrary"),
        name="multiscale_pool",
    )(proj, pool_w, pool_scale.reshape(1, ngroup * gw))


_NT = (((1,), (1,)), ((), ()))


def _dsa_kernel(q_ref, k_ref, vt_ref, iq_ref, ikw_ref, iwt_ref, bias_ref, o_ref,
                kk_ref, keys_ref, cut_ref, mask_ref, logit_ref, acc_ref,
                *, seq, tq, topk, n_heads):
    qt = pl.program_id(1)
    kt_w = tq
    n_pairs = (qt + 2) // 2
    n_tiles = 2 * n_pairs
    n_unseen = seq - n_tiles * kt_w

    @pl.when(qt == 0)
    def _():
        for j in range(seq // kt_w):
            x = ikw_ref[0, j * kt_w:(j + 1) * kt_w, :]
            lane_t = lax.broadcasted_iota(I32, x.shape, 1)
            kk_ref[j, 0:kt_w, :] = jnp.where(lane_t < IDX_DIM, x, 0.0).astype(BF16)
            kk_ref[j, kt_w:2 * kt_w, :] = jnp.where(
                lane_t >= IDX_DIM, pltpu.roll(x, IDX_DIM, axis=1), 0.0).astype(BF16)

    w_all = iwt_ref[0] * ((IDX_HEADS ** -0.5) * (IDX_DIM ** -0.5))
    q_pos = qt * tq + lax.broadcasted_iota(I32, (1, tq), 1)
    vis_end = (q_pos // CHUNK + 1) * CHUNK
    key_row = lax.broadcasted_iota(I32, (kt_w, tq), 0)

    def tile_start(j):
        return pl.multiple_of(j * kt_w, kt_w)

    def score_tile(j, carry):
        s0 = tile_start(j)
        kk = kk_ref[j]
        score = jnp.zeros((kt_w, tq), F32)
        for pair in range(IDX_HEADS // 2):
            a = iq_ref[0, :, pair * LANES:(pair + 1) * LANES]
            d = lax.dot_general(kk, a, _NT, preferred_element_type=F32)
            score = score + w_all[2 * pair:2 * pair + 1] * jnp.maximum(d[:kt_w], 0.0)
            score = score + w_all[2 * pair + 1:2 * pair + 2] * jnp.maximum(d[kt_w:], 0.0)
        bits = pltpu.bitcast(score, I32)
        key = jnp.where(bits >= 0, bits, bits ^ 0x7FFFFFFF)
        key = jnp.where(s0 + key_row < vis_end, key, KEY_NEG_INF)
        keys_ref[pl.ds(s0, kt_w), :] = key
        return carry

    lax.fori_loop(0, n_tiles, score_tile, 0)

    def count_tiles(pred):
        def body(j, acc):
            s0 = tile_start(j)
            kt = keys_ref[pl.ds(s0, kt_w), :]
            hit = jnp.where(pred(kt, s0 + key_row), 1.0, 0.0)
            return acc + jnp.sum(hit.reshape(kt_w // 8, 8, tq), axis=0)
        acc = lax.fori_loop(0, n_tiles, body, jnp.zeros((8, tq), F32))
        return jnp.sum(acc, axis=0, keepdims=True)

    def unseen_if(cond):
        return jnp.where(cond, float(1.0) * n_unseen, 0.0)

    def bit_step(i, thr):
        cand = thr + lax.shift_left(jnp.int32(1), 31 - i)
        cnt = count_tiles(lambda kt, idx: kt >= cand) + unseen_if(cand <= KEY_NEG_INF)
        return jnp.where(cnt >= topk, cand, thr)

    thr = lax.fori_loop(0, 32, bit_step, jnp.full((1, tq), INT_MIN, I32))

    cnt_ge = count_tiles(lambda kt, idx: kt >= thr) + unseen_if(thr <= KEY_NEG_INF)
    cnt_gt = count_tiles(lambda kt, idx: kt > thr) + unseen_if(thr < KEY_NEG_INF)
    need = topk - cnt_gt
    idx_bits = int(seq).bit_length()
    cut_ref[...] = jnp.full((1, tq), 2 ** idx_bits - 1, I32)
    has_tie = jnp.max(jnp.where(cnt_ge != topk, 1.0, 0.0)) > 0.0

    @pl.when(has_tie)
    def _():
        def idx_step(i, cut):
            cand = cut + lax.shift_left(jnp.int32(1), idx_bits - 1 - i)
            below = count_tiles(lambda kt, idx: (kt == thr) & (idx < cand))
            return jnp.where(below <= need, cand, cut)
        cut_ref[...] = lax.fori_loop(0, idx_bits, idx_step, jnp.zeros((1, tq), I32))

    cut = cut_ref[...]

    def mask_tile(j, carry):
        s0 = tile_start(j)
        kt = keys_ref[pl.ds(s0, kt_w), :]
        s_idx = s0 + key_row
        sel = ((kt > thr) | ((kt == thr) & (s_idx < cut))) & (s_idx < vis_end)
        mask_ref[pl.ds(s0, kt_w), :] = jnp.where(sel, 0.0, NEG_INF)
        return carry

    lax.fori_loop(0, n_tiles, mask_tile, 0)

    scale = C_HEAD_DIM ** -0.5

    def fold8(x, op):
        return op(x.reshape(kt_w // 8, 8, tq), axis=0)

    def head_pair_body(hp, carry):
        h0s = [pl.multiple_of((2 * hp + e) * C_HEAD_DIM, C_HEAD_DIM) for e in range(2)]

        def logits_pair(jj, mxs):
            new = []
            for e in range(2):
                qh = q_ref[0, :, pl.ds(h0s[e], C_HEAD_DIM)]
                mx = mxs[e]
                for u in range(2):
                    j = 2 * jj + u
                    s0 = tile_start(j)
                    x0 = pl.multiple_of(jnp.clip(j - qt + 2, 0, 2) * kt_w, kt_w)
                    lg = lax.dot_general(k_ref[0, pl.ds(s0, kt_w), pl.ds(h0s[e], C_HEAD_DIM)], qh, _NT,
                                         preferred_element_type=F32)
                    lg = lg * scale + bias_ref[2 * hp + e, pl.ds(x0, kt_w), :] + mask_ref[pl.ds(s0, kt_w), :]
                    logit_ref[e, pl.ds(s0, kt_w), :] = lg
                    mx = jnp.maximum(mx, fold8(lg, jnp.max))
                new.append(mx)
            return tuple(new)

        neg = jnp.full((8, tq), NEG_INF, F32)
        mxs = lax.fori_loop(0, n_pairs, logits_pair, (neg, neg))
        ms = [jnp.max(mx, axis=0, keepdims=True) for mx in mxs]
        acc_ref[...] = jnp.zeros(acc_ref.shape, F32)

        def weigh_pair(jj, ls):
            s0 = pl.multiple_of(jj * 2 * kt_w, 2 * kt_w)
            new = []
            for e in range(2):
                p = jnp.exp(logit_ref[e, pl.ds(s0, 2 * kt_w), :] - ms[e])
                acc_ref[e] += jnp.dot(vt_ref[0, pl.ds(h0s[e], C_HEAD_DIM), pl.ds(s0, 2 * kt_w)],
                                      p.astype(BF16), preferred_element_type=F32)
                new.append(ls[e] + jnp.sum(p.reshape(2 * kt_w // 8, 8, tq), axis=0))
            return tuple(new)

        zero = jnp.zeros((8, tq), F32)
        ls = lax.fori_loop(0, n_pairs, weigh_pair, (zero, zero))
        for e in range(2):
            out = acc_ref[e] / jnp.sum(ls[e], axis=0, keepdims=True)
            o_ref[0, :, pl.ds(h0s[e], C_HEAD_DIM)] = out.T.astype(o_ref.dtype)
        return carry

    lax.fori_loop(0, n_heads // 2, head_pair_body, 0)


def _t5_bucket(rel):
    nb = NUM_BUCKETS // 2
    max_exact = nb // 2
    ret = jnp.where(rel > 0, nb, 0)
    n = jnp.abs(rel)
    large = max_exact + (jnp.log(jnp.maximum(n, 1).astype(F32) / max_exact)
                         / math.log(MAX_DISTANCE / max_exact) * (nb - max_exact)).astype(I32)
    large = jnp.minimum(large, nb - 1)
    return ret + jnp.where(n < max_exact, n, large)


def _bias_slab(rel_bias_table, tq):
    assert tq >= MAX_DISTANCE
    rel = jnp.arange(3 * tq, dtype=I32)[:, None] - 2 * tq - jnp.arange(tq, dtype=I32)[None, :]
    bucket = _t5_bucket(rel)[None]
    table = rel_bias_table.astype(F32)
    slab = jnp.zeros((table.shape[1], 3 * tq, tq), F32)
    for b in range(NUM_BUCKETS):
        slab = jnp.where(bucket == b, table[b][:, None, None], slab)
    return slab


def _dsa_attention(proj_qkv, proj_rest, rel_bias_table, *, c_width, rest_idx_block, tq):
    bsz, seq, _ = proj_qkv.shape
    n_heads = c_width // C_HEAD_DIM
    topk = min(TOPK_MAX, seq // 4)
    assert (seq // tq) % 2 == 0 and n_heads % 2 == 0
    slab = _bias_slab(rel_bias_table, tq)
    v_t = jnp.swapaxes(proj_qkv[..., 2 * c_width:3 * c_width], 1, 2)
    w0 = rest_idx_block * LANES + IDX_DIM
    iw_t = jnp.swapaxes(proj_rest[..., w0:w0 + IDX_HEADS], 1, 2)
    kernel = functools.partial(_dsa_kernel, seq=seq, tq=tq, topk=topk, n_heads=n_heads)
    return pl.pallas_call(
        kernel,
        grid=(bsz, seq // tq),
        in_specs=[
            pl.BlockSpec((1, tq, c_width), lambda b, t: (b, t, 0)),
            pl.BlockSpec((1, seq, c_width), lambda b, t: (b, 0, 1), pipeline_mode=pl.Buffered(1)),
            pl.BlockSpec((1, c_width, seq), lambda b, t: (b, 0, 0), pipeline_mode=pl.Buffered(1)),
            pl.BlockSpec((1, tq, IDX_HEADS * IDX_DIM), lambda b, t: (b, t, 3)),
            pl.BlockSpec((1, seq, LANES), lambda b, t: (b, 0, rest_idx_block), pipeline_mode=pl.Buffered(1)),
            pl.BlockSpec((1, IDX_HEADS, tq), lambda b, t: (b, 0, t)),
            _resident((n_heads, 3 * tq, tq), lambda b, t: (0, 0, 0)),
        ],
        out_specs=pl.BlockSpec((1, tq, c_width), lambda b, t: (b, t, 0)),
        out_shape=jax.ShapeDtypeStruct((bsz, seq, c_width), BF16),
        scratch_shapes=[
            pltpu.VMEM((seq // tq, 2 * tq, LANES), BF16),
            pltpu.VMEM((seq, tq), I32),
            pltpu.VMEM((1, tq), I32),
            pltpu.VMEM((seq, tq), F32),
            pltpu.VMEM((2, seq, tq), F32),
            pltpu.VMEM((2, C_HEAD_DIM, tq), F32),
        ],
        compiler_params=_params("arbitrary", "arbitrary"),
        name="dsa_attention",
    )(proj_qkv, proj_qkv, v_t, proj_qkv, proj_rest, iw_t, slab)


def _rope_pairs(x, cos, sin_signed):
    lane = lax.broadcasted_iota(I32, x.shape, 1)
    half = R_QK_DIM // 2
    swapped = jnp.where((lane % R_QK_DIM) < half,
                        pltpu.roll(x, LANES - half, axis=1), pltpu.roll(x, half, axis=1))
    return x * cos + swapped * sin_signed


def _retention_kernel(q_ref, k_ref, v_ref, g_ref, cos_ref, sin_ref, gain_ref, o_ref,
                      state_ref, *, blk, n_heads):
    tb = pl.program_id(1)

    @pl.when(tb == 0)
    def _():
        state_ref[...] = jnp.zeros(state_ref.shape, F32)

    lane = lax.broadcasted_iota(I32, (blk, LANES), 1)
    ri = lax.broadcasted_iota(I32, (blk, blk), 0)
    ci = lax.broadcasted_iota(I32, (blk, blk), 1)
    visible = (ci // CHUNK) <= (ri // CHUNK)
    dist = jnp.abs(ri - ci).astype(F32)
    pos = lax.broadcasted_iota(I32, (blk, 1), 0).astype(F32)

    for pair in range(n_heads // 2):
        ls = slice(pair * LANES, (pair + 1) * LANES)
        qr = _rope_pairs(q_ref[0, :, ls], cos_ref[:, ls], sin_ref[:, ls])
        kr = _rope_pairs(k_ref[0, :, ls], cos_ref[:, ls], sin_ref[:, ls]) * (R_QK_DIM ** -0.5)
        kr_b = kr.astype(BF16)
        for e in range(2):
            hd = 2 * pair + e
            vs = slice(hd * R_V_DIM, (hd + 1) * R_V_DIM)
            log_g = jnp.log(jnp.full((1, 1), 1.0 - 2.0 ** (-5.0 - hd), F32))
            mine = (lane >= e * R_QK_DIM) & (lane < (e + 1) * R_QK_DIM)
            qm = jnp.where(mine, qr, 0.0)
            km = jnp.where(mine, kr, 0.0)
            v_b = v_ref[0, :, vs].astype(BF16)
            decay = jnp.where(visible, jnp.exp(dist * log_g), 0.0)
            a = lax.dot_general(qm.astype(BF16), kr_b, (((1,), (1,)), ((), ())),
                                preferred_element_type=F32) * decay
            o = jnp.dot(a.astype(BF16), v_b, preferred_element_type=F32)
            state = state_ref[hd]
            q_dec = qm * jnp.exp((pos + 1.0) * log_g)
            o = o + jnp.dot(q_dec.astype(BF16), state.astype(BF16), preferred_element_type=F32)
            k_dec = km * jnp.exp((blk - 1.0 - pos) * log_g)
            kv = lax.dot_general(k_dec.astype(BF16), v_b, (((0,), (0,)), ((), ())),
                                 preferred_element_type=F32)
            state_ref[hd] = state * jnp.exp(blk * log_g) + kv
            mu = jnp.mean(o, axis=-1, keepdims=True)
            var = jnp.mean(jnp.square(o - mu), axis=-1, keepdims=True)
            on = (o - mu) * lax.rsqrt(var + LN_EPS) * gain_ref[:, vs]
            gate = g_ref[0, :, vs]
            o_ref[0, :, vs] = (gate * jax.nn.sigmoid(gate) * on).astype(o_ref.dtype)


def _rope_tables(seq, n_heads):
    half = R_QK_DIM // 2
    freqs = ROPE_BASE ** (-jnp.arange(half, dtype=F32) / half)
    ang = jnp.arange(seq).astype(F32)[:, None] * freqs[None, :]
    cos, sin = jnp.cos(ang), jnp.sin(ang)
    cos_t = jnp.tile(jnp.concatenate([cos, cos], axis=1), (1, n_heads))
    sin_t = jnp.tile(jnp.concatenate([-sin, sin], axis=1), (1, n_heads))
    return cos_t, sin_t


def _retention(proj_rest, ret_gain, *, r_width, blk):
    bsz, seq, _ = proj_rest.shape
    n_heads = r_width // R_V_DIM
    qk_width = n_heads * R_QK_DIM
    cos_t, sin_t = _rope_tables(seq, n_heads)
    kernel = functools.partial(_retention_kernel, blk=blk, n_heads=n_heads)
    return pl.pallas_call(
        kernel,
        grid=(bsz, seq // blk),
        in_specs=[
            pl.BlockSpec((1, blk, qk_width), lambda b, t: (b, t, 0)),
            pl.BlockSpec((1, blk, qk_width), lambda b, t: (b, t, 1)),
            pl.BlockSpec((1, blk, r_width), lambda b, t: (b, t, 1)),
            pl.BlockSpec((1, blk, r_width), lambda b, t: (b, t, 2)),
            pl.BlockSpec((blk, qk_width), lambda b, t: (t, 0)),
            pl.BlockSpec((blk, qk_width), lambda b, t: (t, 0)),
            _resident((1, r_width), lambda b, t: (0, 0)),
        ],
        out_specs=pl.BlockSpec((1, blk, r_width), lambda b, t: (b, t, 0)),
        out_shape=jax.ShapeDtypeStruct((bsz, seq, r_width), BF16),
        scratch_shapes=[pltpu.VMEM((n_heads, LANES, R_V_DIM), F32)],
        compiler_params=_params("arbitrary", "arbitrary"),
        name="retention",
    )(proj_rest, proj_rest, proj_rest, proj_rest, cos_t, sin_t, ret_gain.reshape(1, r_width))


def _largest_tile(n, cap, multiple=LANES):
    best = None
    for t in range(multiple, min(n, cap) + 1, multiple):
        if n % t == 0:
            best = t
    assert best is not None, (n, cap)
    return best


def kernel(x, norm_gains, ffn_w_gate_up, ffn_w_down, ab_w_in, ab_conv_w, ab_pool_w, ab_pool_scale,
           ab_w_out, cd_w_in, ret_norm_gain, cd_w_out, rel_bias_table, final_norm):
    bsz, seq, d = x.shape
    m = bsz * seq
    depth = norm_gains.shape[0]
    a_width = ab_conv_w.shape[-1]
    b_width = ab_pool_scale.shape[-1]
    c_width = cd_w_out.shape[1] // 2
    r_width = ret_norm_gain.shape[-1]
    qk_width = (r_width // R_V_DIM) * R_QK_DIM
    assert c_width == IDX_HEADS * IDX_DIM == r_width and 2 * qk_width == r_width

    tm = _largest_tile(m, 512, 8)
    tf = _largest_tile(ffn_w_down.shape[2], 512)
    seq_tile = _largest_tile(seq, 256, CHUNK)
    pool_rows = _largest_tile(seq, 512, 8)

    w_gu = ffn_w_gate_up.astype(BF16)
    w_dn = ffn_w_down.astype(BF16)

    h = x.reshape(m, d)
    for layer in range(depth):
        j = layer // 2
        h = _ffn(h, norm_gains[layer, 0], w_gu, w_dn, layer, 0, tm=tm, tf=tf)
        if layer % 2 == 0:
            w_in = ab_w_in[j].astype(BF16)
            proj = _norm_matmul(h, norm_gains[layer, 1], w_in, F32,
                                tm=tm, tn=_largest_tile(w_in.shape[1], 2048))
            proj = proj.reshape(bsz, seq, -1)
            y1 = _gated_conv(proj, ab_conv_w[j], cb=2 * LANES, rows=pool_rows)
            y2 = _multiscale_pool(proj, ab_pool_w[j].astype(BF16), ab_pool_scale[j],
                                  col0=3 * a_width, rows=pool_rows)
            w_out = ab_w_out[j].astype(BF16)
        else:
            w_in = cd_w_in[j]
            o_iq = 3 * c_width
            o_ik = o_iq + IDX_HEADS * IDX_DIM
            o_rq = o_ik + IDX_DIM + IDX_HEADS
            n_rest = 2 * qk_width + 2 * r_width
            w_qkv = w_in[:, :o_ik].astype(BF16)
            pad = LANES - (IDX_DIM + IDX_HEADS)
            w_rest = jnp.concatenate(
                [w_in[:, o_rq:o_rq + n_rest], w_in[:, o_ik:o_rq], jnp.zeros((d, pad), F32)],
                axis=1).astype(BF16)
            proj_qkv = _norm_matmul(h, norm_gains[layer, 1], w_qkv, BF16,
                                    tm=tm, tn=_largest_tile(w_qkv.shape[1], 2048))
            proj_rest = _norm_matmul(h, norm_gains[layer, 1], w_rest, F32,
                                     tm=tm, tn=_largest_tile(w_rest.shape[1], 3200))
            y1 = _dsa_attention(proj_qkv.reshape(bsz, seq, -1), proj_rest.reshape(bsz, seq, -1),
                                rel_bias_table, c_width=c_width,
                                rest_idx_block=n_rest // LANES, tq=seq_tile)
            y2 = _retention(proj_rest.reshape(bsz, seq, -1), ret_norm_gain[j],
                            r_width=r_width, blk=seq_tile)
            w_out = cd_w_out[j].astype(BF16)
        h = _out_proj(h, y1.reshape(m, -1), y2.reshape(m, -1), w_out, tm=tm)
        h = _ffn(h, norm_gains[layer, 2], w_gu, w_dn, layer, 1, tm=tm, tf=tf,
                 closing_gain=final_norm if layer == depth - 1 else None)
    return h.reshape(bsz, seq, d)
```

```python
import functools
import math

import numpy as np
import jax
import jax.numpy as jnp
from jax import lax
from jax.experimental import pallas as pl
from jax.experimental.pallas import tpu as pltpu

F32 = jnp.float32
BF16 = jnp.bfloat16
I32 = jnp.int32

LANES = 128
VMEM_LIMIT_BYTES = 56 * 1024 * 1024

CHUNK = 64
RMS_EPS = 1e-6
LN_EPS = 1e-5
NEG_INF = -1e30
CONV_W = 3
POOL_WINDOWS = (2, 4, 8, 16)
POOL_HALO = 16
C_HEAD_DIM = 128
IDX_HEADS = 16
IDX_DIM = 64
TOPK_MAX = 256
NUM_BUCKETS = 32
MAX_DISTANCE = 128
R_QK_DIM = 64
R_V_DIM = 128
ROPE_BASE = 10000.0

INT_MIN = -2 ** 31


def _sortable_key_of(value):
    bits = int(np.array(value, np.float32).view(np.int32))
    return bits if bits >= 0 else bits ^ 0x7FFFFFFF


KEY_NEG_INF = _sortable_key_of(NEG_INF)


def _params(*semantics):
    return pltpu.CompilerParams(dimension_semantics=semantics, vmem_limit_bytes=VMEM_LIMIT_BYTES)


def _resident(block_shape, index_map):
    return pl.BlockSpec(block_shape, index_map, pipeline_mode=pl.Buffered(1))


def _rmsnorm_rows(x, gain):
    y = x * lax.rsqrt(jnp.mean(x * x, axis=-1, keepdims=True) + RMS_EPS)
    return y * gain


def _ffn_kernel(h_ref, g_ref, wg_ref, wu_ref, wd_ref, *rest, n_hidden_tiles, closing_norm):
    if closing_norm:
        fg_ref, o_ref, xn_ref = rest
    else:
        o_ref, xn_ref = rest
    f = pl.program_id(1)

    @pl.when(f == 0)
    def _():
        x = h_ref[...]
        xn_ref[...] = _rmsnorm_rows(x, g_ref[...]).astype(BF16)
        o_ref[...] = x

    xn = xn_ref[...]
    gate = jnp.dot(xn, wg_ref[...], preferred_element_type=F32)
    up = jnp.dot(xn, wu_ref[...], preferred_element_type=F32)
    act = (0.5 * (gate * jax.nn.sigmoid(gate)) * up).astype(BF16)
    o_ref[...] += jnp.dot(act, wd_ref[...], preferred_element_type=F32)

    if closing_norm:
        @pl.when(f == n_hidden_tiles - 1)
        def _():
            o_ref[...] = _rmsnorm_rows(o_ref[...], fg_ref[...])


def _ffn(h, gain, w_gate_up, w_down, layer, which, *, tm, tf, closing_gain=None):
    m, d = h.shape
    d_ff = w_down.shape[2]
    nf = d_ff // tf
    closing_norm = closing_gain is not None
    in_specs = [
        pl.BlockSpec((tm, d), lambda i, f: (i, 0)),
        _resident((1, d), lambda i, f: (0, 0)),
        pl.BlockSpec((None, None, d, tf), lambda i, f: (layer, which, 0, f)),
        pl.BlockSpec((None, None, d, tf), lambda i, f: (layer, which, 0, f + nf)),
        pl.BlockSpec((None, None, tf, d), lambda i, f: (layer, which, f, 0)),
    ]
    args = [h, gain.reshape(1, d), w_gate_up, w_gate_up, w_down]
    if closing_norm:
        in_specs.append(_resident((1, d), lambda i, f: (0, 0)))
        args.append(closing_gain.reshape(1, d))
    return pl.pallas_call(
        functools.partial(_ffn_kernel, n_hidden_tiles=nf, closing_norm=closing_norm),
        grid=(m // tm, nf),
        in_specs=in_specs,
        out_specs=pl.BlockSpec((tm, d), lambda i, f: (i, 0)),
        out_shape=jax.ShapeDtypeStruct((m, d), F32),
        scratch_shapes=[pltpu.VMEM((tm, d), BF16)],
        compiler_params=_params("arbitrary", "arbitrary"),
        name="ffn",
    )(*args)


def _norm_matmul_kernel(h_ref, g_ref, w_ref, o_ref):
    xn = _rmsnorm_rows(h_ref[...], g_ref[...]).astype(BF16)
    o_ref[...] = jnp.dot(xn, w_ref[...], preferred_element_type=F32).astype(o_ref.dtype)


def _norm_matmul(h, gain, w, out_dtype, *, tm, tn):
    m, d = h.shape
    n = w.shape[1]
    return pl.pallas_call(
        _norm_matmul_kernel,
        grid=(n // tn, m // tm),
        in_specs=[
            pl.BlockSpec((tm, d), lambda j, i: (i, 0)),
            _resident((1, d), lambda j, i: (0, 0)),
            pl.BlockSpec((d, tn), lambda j, i: (0, j)),
        ],
        out_specs=pl.BlockSpec((tm, tn), lambda j, i: (i, j)),
        out_shape=jax.ShapeDtypeStruct((m, n), out_dtype),
        compiler_params=_params("arbitrary", "arbitrary"),
        name="norm_matmul",
    )(h, gain.reshape(1, d), w)


def _out_proj_kernel(h_ref, y1_ref, y2_ref, w1_ref, w2_ref, o_ref):
    acc = jnp.dot(y1_ref[...], w1_ref[...], preferred_element_type=F32)
    acc += jnp.dot(y2_ref[...], w2_ref[...], preferred_element_type=F32)
    o_ref[...] = h_ref[...] + acc


def _out_proj(h, y1, y2, w, *, tm):
    m, d = h.shape
    half = y1.shape[1]
    return pl.pallas_call(
        _out_proj_kernel,
        grid=(m // tm,),
        in_specs=[
            pl.BlockSpec((tm, d), lambda i: (i, 0)),
            pl.BlockSpec((tm, half), lambda i: (i, 0)),
            pl.BlockSpec((tm, half), lambda i: (i, 0)),
            _resident((half, d), lambda i: (0, 0)),
            _resident((half, d), lambda i: (1, 0)),
        ],
        out_specs=pl.BlockSpec((tm, d), lambda i: (i, 0)),
        out_shape=jax.ShapeDtypeStruct((m, d), F32),
        compiler_params=_params("arbitrary"),
        name="out_proj",
    )(h, y1, y2, w, w)


def _rows_with_halo(load, r0, rows, width):
    if r0 == 0:
        return jnp.concatenate([jnp.zeros((POOL_HALO, width), F32), load(0, rows)], axis=0)
    return load(r0 - POOL_HALO, rows + POOL_HALO)


def _conv_kernel(b_ref, c_ref, v_ref, w_ref, o_ref, *, rows):
    t, cb = b_ref.shape[1], b_ref.shape[2]
    w0, w1, w2 = w_ref[0:1, :], w_ref[1:2, :], w_ref[2:3, :]
    for r0 in range(0, t, rows):
        ext = _rows_with_halo(lambda s, n: c_ref[0, s:s + n, :] * v_ref[0, s:s + n, :], r0, rows, cb)
        u0 = ext[POOL_HALO:]
        u1 = pltpu.roll(ext, 1, axis=0)[POOL_HALO:]
        u2 = pltpu.roll(ext, 2, axis=0)[POOL_HALO:]
        y = w0 * u2 + w1 * u1 + w2 * u0
        o_ref[0, r0:r0 + rows, :] = (b_ref[0, r0:r0 + rows, :] * y).astype(o_ref.dtype)


def _gated_conv(proj, conv_w, *, cb, rows):
    bsz, t, _ = proj.shape
    a_width = conv_w.shape[1]
    ncb = a_width // cb
    return pl.pallas_call(
        functools.partial(_conv_kernel, rows=rows),
        grid=(bsz, ncb),
        in_specs=[
            pl.BlockSpec((1, t, cb), lambda b, c: (b, 0, c)),
            pl.BlockSpec((1, t, cb), lambda b, c: (b, 0, c + ncb)),
            pl.BlockSpec((1, t, cb), lambda b, c: (b, 0, c + 2 * ncb)),
            pl.BlockSpec((CONV_W, cb), lambda b, c: (0, c)),
        ],
        out_specs=pl.BlockSpec((1, t, cb), lambda b, c: (b, 0, c)),
        out_shape=jax.ShapeDtypeStruct((bsz, t, a_width), BF16),
        compiler_params=_params("arbitrary", "arbitrary"),
        name="gated_conv",
    )(proj, proj, proj, conv_w)


def _pool_kernel(x_ref, w_ref, s_ref, o_ref, *, rows):
    t, gw = x_ref.shape[1], x_ref.shape[2]
    group = pl.program_id(1)
    wmat = w_ref[0]
    scale = s_ref[...]
    for gi, window in enumerate(POOL_WINDOWS):
        @pl.when(group == gi)
        def _(window=window):
            for r0 in range(0, t, rows):
                ext = _rows_with_halo(lambda s, n: x_ref[0, s:s + n, :], r0, rows, gw)
                acc = ext
                span = 1
                while span < window:
                    acc = acc + pltpu.roll(acc, span, axis=0)
                    span *= 2
                pos = r0 + lax.broadcasted_iota(I32, (rows, 1), 0)
                count = jnp.minimum(pos + 1, window).astype(F32)
                pooled = acc[POOL_HALO:] / count - ext[POOL_HALO:]
                mixed = jnp.dot(pooled.astype(BF16), wmat, preferred_element_type=F32)
                o_ref[0, r0:r0 + rows, :] = (mixed * scale).astype(o_ref.dtype)


def _multiscale_pool(proj, pool_w, pool_scale, *, col0, rows):
    bsz, t, _ = proj.shape
    ngroup, gw, _ = pool_w.shape
    cblk0 = col0 // gw
    return pl.pallas_call(
        functools.partial(_pool_kernel, rows=rows),
        grid=(bsz, ngroup),
        in_specs=[
            pl.BlockSpec((1, t, gw), lambda b, g: (b, 0, cblk0 + g)),
            pl.BlockSpec((1, gw, gw), lambda b, g: (g, 0, 0)),
            pl.BlockSpec((1, gw), lambda b, g: (0, g)),
        ],
        out_specs=pl.BlockSpec((1, t, gw), lambda b, g: (b, 0, g)),
        out_shape=jax.ShapeDtypeStruct((bsz, t, ngroup * gw), BF16),
        compiler_params=_params("arbitrary", "arbitrary"),
        name="multiscale_pool",
    )(proj, pool_w, pool_scale.reshape(1, ngroup * gw))


_NT = (((1,), (1,)), ((), ()))


def _dsa_kernel(q_ref, k_ref, vt_ref, iq_ref, ikw_ref, iwt_ref, bias_ref, o_ref,
                kk_ref, keys_ref, cut_ref, mask_ref, logit_ref, acc_ref,
                *, seq, tq, topk, n_heads):
    qt = pl.program_id(1)
    kt_w = tq
    n_pairs = (qt + 2) // 2
    n_tiles = 2 * n_pairs
    n_unseen = seq - n_tiles * kt_w

    @pl.when(qt == 0)
    def _():
        for j in range(seq // kt_w):
            x = ikw_ref[0, j * kt_w:(j + 1) * kt_w, :]
            lane_t = lax.broadcasted_iota(I32, x.shape, 1)
            kk_ref[j, 0:kt_w, :] = jnp.where(lane_t < IDX_DIM, x, 0.0).astype(BF16)
            kk_ref[j, kt_w:2 * kt_w, :] = jnp.where(
                lane_t >= IDX_DIM, pltpu.roll(x, IDX_DIM, axis=1), 0.0).astype(BF16)

    w_all = iwt_ref[0] * ((IDX_HEADS ** -0.5) * (IDX_DIM ** -0.5))
    q_pos = qt * tq + lax.broadcasted_iota(I32, (1, tq), 1)
    vis_end = (q_pos // CHUNK + 1) * CHUNK
    key_row = lax.broadcasted_iota(I32, (kt_w, tq), 0)

    def tile_start(j):
        return pl.multiple_of(j * kt_w, kt_w)

    def score_tile(j, carry):
        s0 = tile_start(j)
        kk = kk_ref[j]
        score = jnp.zeros((kt_w, tq), F32)
        for pair in range(IDX_HEADS // 2):
            a = iq_ref[0, :, pair * LANES:(pair + 1) * LANES]
            d = lax.dot_general(kk, a, _NT, preferred_element_type=F32)
            score = score + w_all[2 * pair:2 * pair + 1] * jnp.maximum(d[:kt_w], 0.0)
            score = score + w_all[2 * pair + 1:2 * pair + 2] * jnp.maximum(d[kt_w:], 0.0)
        bits = pltpu.bitcast(score, I32)
        key = jnp.where(bits >= 0, bits, bits ^ 0x7FFFFFFF)
        key = jnp.where(s0 + key_row < vis_end, key, KEY_NEG_INF)
        keys_ref[pl.ds(s0, kt_w), :] = key
        return carry

    lax.fori_loop(0, n_tiles, score_tile, 0)

    def count_tiles(pred):
        def body(jj, accs):
            new = []
            for u in range(2):
                s0 = tile_start(2 * jj + u)
                kt = keys_ref[pl.ds(s0, kt_w), :]
                hit = jnp.where(pred(kt, s0 + key_row), 1.0, 0.0)
                new.append(accs[u] + jnp.sum(hit.reshape(kt_w // 8, 8, tq), axis=0))
            return tuple(new)
        zero = jnp.zeros((8, tq), F32)
        acc0, acc1 = lax.fori_loop(0, n_pairs, body, (zero, zero))
        return jnp.sum(acc0 + acc1, axis=0, keepdims=True)

    def unseen_if(cond):
        return jnp.where(cond, float(1.0) * n_unseen, 0.0)

    def bit_step(i, thr):
        cand = thr + lax.shift_left(jnp.int32(1), 31 - i)
        cnt = count_tiles(lambda kt, idx: kt >= cand) + unseen_if(cand <= KEY_NEG_INF)
        return jnp.where(cnt >= topk, cand, thr)

    thr = lax.fori_loop(0, 32, bit_step, jnp.full((1, tq), INT_MIN, I32))

    cnt_ge = count_tiles(lambda kt, idx: kt >= thr) + unseen_if(thr <= KEY_NEG_INF)
    cnt_gt = count_tiles(lambda kt, idx: kt > thr) + unseen_if(thr < KEY_NEG_INF)
    need = topk - cnt_gt
    idx_bits = int(seq).bit_length()
    cut_ref[...] = jnp.full((1, tq), 2 ** idx_bits - 1, I32)
    has_tie = jnp.max(jnp.where(cnt_ge != topk, 1.0, 0.0)) > 0.0

    @pl.when(has_tie)
    def _():
        def idx_step(i, cut):
            cand = cut + lax.shift_left(jnp.int32(1), idx_bits - 1 - i)
            below = count_tiles(lambda kt, idx: (kt == thr) & (idx < cand))
            return jnp.where(below <= need, cand, cut)
        cut_ref[...] = lax.fori_loop(0, idx_bits, idx_step, jnp.zeros((1, tq), I32))

    cut = cut_ref[...]

    def mask_tile(j, carry):
        s0 = tile_start(j)
        kt = keys_ref[pl.ds(s0, kt_w), :]
        s_idx = s0 + key_row
        sel = ((kt > thr) | ((kt == thr) & (s_idx < cut))) & (s_idx < vis_end)
        mask_ref[pl.ds(s0, kt_w), :] = jnp.where(sel, 0.0, NEG_INF)
        return carry

    lax.fori_loop(0, n_tiles, mask_tile, 0)

    scale = C_HEAD_DIM ** -0.5

    def fold8(x, op):
        return op(x.reshape(kt_w // 8, 8, tq), axis=0)

    def head_pair_body(hp, carry):
        h0s = [pl.multiple_of((2 * hp + e) * C_HEAD_DIM, C_HEAD_DIM) for e in range(2)]

        def logits_pair(jj, mxs):
            new = []
            for e in range(2):
                qh = q_ref[0, :, pl.ds(h0s[e], C_HEAD_DIM)]
                mx = mxs[e]
                for u in range(2):
                    j = 2 * jj + u
                    s0 = tile_start(j)
                    x0 = pl.multiple_of(jnp.clip(j - qt + 2, 0, 2) * kt_w, kt_w)
                    lg = lax.dot_general(k_ref[0, pl.ds(s0, kt_w), pl.ds(h0s[e], C_HEAD_DIM)], qh, _NT,
                                         preferred_element_type=F32)
                    lg = lg * scale + bias_ref[2 * hp + e, pl.ds(x0, kt_w), :] + mask_ref[pl.ds(s0, kt_w), :]
                    logit_ref[e, pl.ds(s0, kt_w), :] = lg
                    mx = jnp.maximum(mx, fold8(lg, jnp.max))
                new.append(mx)
            return tuple(new)

        neg = jnp.full((8, tq), NEG_INF, F32)
        mxs = lax.fori_loop(0, n_pairs, logits_pair, (neg, neg))
        ms = [jnp.max(mx, axis=0, keepdims=True) for mx in mxs]
        acc_ref[...] = jnp.zeros(acc_ref.shape, F32)

        def weigh_pair(jj, ls):
            s0 = pl.multiple_of(jj * 2 * kt_w, 2 * kt_w)
            new = []
            for e in range(2):
                p = jnp.exp(logit_ref[e, pl.ds(s0, 2 * kt_w), :] - ms[e])
                acc_ref[e] += jnp.dot(vt_ref[0, pl.ds(h0s[e], C_HEAD_DIM), pl.ds(s0, 2 * kt_w)],
                                      p.astype(BF16), preferred_element_type=F32)
                new.append(ls[e] + jnp.sum(p.reshape(2 * kt_w // 8, 8, tq), axis=0))
            return tuple(new)

        zero = jnp.zeros((8, tq), F32)
        ls = lax.fori_loop(0, n_pairs, weigh_pair, (zero, zero))
        for e in range(2):
            out = acc_ref[e] / jnp.sum(ls[e], axis=0, keepdims=True)
            o_ref[0, :, pl.ds(h0s[e], C_HEAD_DIM)] = out.T.astype(o_ref.dtype)
        return carry

    lax.fori_loop(0, n_heads // 2, head_pair_body, 0)


def _t5_bucket(rel):
    nb = NUM_BUCKETS // 2
    max_exact = nb // 2
    ret = jnp.where(rel > 0, nb, 0)
    n = jnp.abs(rel)
    large = max_exact + (jnp.log(jnp.maximum(n, 1).astype(F32) / max_exact)
                         / math.log(MAX_DISTANCE / max_exact) * (nb - max_exact)).astype(I32)
    large = jnp.minimum(large, nb - 1)
    return ret + jnp.where(n < max_exact, n, large)


def _bias_slab(rel_bias_table, tq):
    assert tq >= MAX_DISTANCE
    rel = jnp.arange(3 * tq, dtype=I32)[:, None] - 2 * tq - jnp.arange(tq, dtype=I32)[None, :]
    bucket = _t5_bucket(rel)[None]
    table = rel_bias_table.astype(F32)
    slab = jnp.zeros((table.shape[1], 3 * tq, tq), F32)
    for b in range(NUM_BUCKETS):
        slab = jnp.where(bucket == b, table[b][:, None, None], slab)
    return slab


def _dsa_attention(proj_qkv, proj_rest, rel_bias_table, *, c_width, rest_idx_block, tq):
    bsz, seq, _ = proj_qkv.shape
    n_heads = c_width // C_HEAD_DIM
    topk = min(TOPK_MAX, seq // 4)
    assert (seq // tq) % 2 == 0 and n_heads % 2 == 0
    slab = _bias_slab(rel_bias_table, tq)
    v_t = jnp.swapaxes(proj_qkv[..., 2 * c_width:3 * c_width], 1, 2)
    w0 = rest_idx_block * LANES + IDX_DIM
    iw_t = jnp.swapaxes(proj_rest[..., w0:w0 + IDX_HEADS], 1, 2)
    kernel = functools.partial(_dsa_kernel, seq=seq, tq=tq, topk=topk, n_heads=n_heads)
    return pl.pallas_call(
        kernel,
        grid=(bsz, seq // tq),
        in_specs=[
            pl.BlockSpec((1, tq, c_width), lambda b, t: (b, t, 0)),
            pl.BlockSpec((1, seq, c_width), lambda b, t: (b, 0, 1), pipeline_mode=pl.Buffered(1)),
            pl.BlockSpec((1, c_width, seq), lambda b, t: (b, 0, 0), pipeline_mode=pl.Buffered(1)),
            pl.BlockSpec((1, tq, IDX_HEADS * IDX_DIM), lambda b, t: (b, t, 3)),
            pl.BlockSpec((1, seq, LANES), lambda b, t: (b, 0, rest_idx_block), pipeline_mode=pl.Buffered(1)),
            pl.BlockSpec((1, IDX_HEADS, tq), lambda b, t: (b, 0, t)),
            _resident((n_heads, 3 * tq, tq), lambda b, t: (0, 0, 0)),
        ],
        out_specs=pl.BlockSpec((1, tq, c_width), lambda b, t: (b, t, 0)),
        out_shape=jax.ShapeDtypeStruct((bsz, seq, c_width), BF16),
        scratch_shapes=[
            pltpu.VMEM((seq // tq, 2 * tq, LANES), BF16),
            pltpu.VMEM((seq, tq), I32),
            pltpu.VMEM((1, tq), I32),
            pltpu.VMEM((seq, tq), F32),
            pltpu.VMEM((2, seq, tq), F32),
            pltpu.VMEM((2, C_HEAD_DIM, tq), F32),
        ],
        compiler_params=_params("arbitrary", "arbitrary"),
        name="dsa_attention",
    )(proj_qkv, proj_qkv, v_t, proj_qkv, proj_rest, iw_t, slab)


def _rope_pairs(x, cos, sin_signed):
    lane = lax.broadcasted_iota(I32, x.shape, 1)
    half = R_QK_DIM // 2
    swapped = jnp.where((lane % R_QK_DIM) < half,
                        pltpu.roll(x, LANES - half, axis=1), pltpu.roll(x, half, axis=1))
    return x * cos + swapped * sin_signed


def _retention_kernel(q_ref, k_ref, v_ref, g_ref, cos_ref, sin_ref, gain_ref, o_ref,
                      state_ref, *, blk, n_heads):
    tb = pl.program_id(1)

    @pl.when(tb == 0)
    def _():
        state_ref[...] = jnp.zeros(state_ref.shape, F32)

    lane = lax.broadcasted_iota(I32, (blk, LANES), 1)
    ri = lax.broadcasted_iota(I32, (blk, blk), 0)
    ci = lax.broadcasted_iota(I32, (blk, blk), 1)
    visible = (ci // CHUNK) <= (ri // CHUNK)
    dist = jnp.abs(ri - ci).astype(F32)
    pos = lax.broadcasted_iota(I32, (blk, 1), 0).astype(F32)

    for pair in range(n_heads // 2):
        ls = slice(pair * LANES, (pair + 1) * LANES)
        qr = _rope_pairs(q_ref[0, :, ls], cos_ref[:, ls], sin_ref[:, ls])
        kr = _rope_pairs(k_ref[0, :, ls], cos_ref[:, ls], sin_ref[:, ls]) * (R_QK_DIM ** -0.5)
        kr_b = kr.astype(BF16)
        for e in range(2):
            hd = 2 * pair + e
            vs = slice(hd * R_V_DIM, (hd + 1) * R_V_DIM)
            log_g = jnp.log(jnp.full((1, 1), 1.0 - 2.0 ** (-5.0 - hd), F32))
            mine = (lane >= e * R_QK_DIM) & (lane < (e + 1) * R_QK_DIM)
            qm = jnp.where(mine, qr, 0.0)
            km = jnp.where(mine, kr, 0.0)
            v_b = v_ref[0, :, vs].astype(BF16)
            decay = jnp.where(visible, jnp.exp(dist * log_g), 0.0)
            a = lax.dot_general(qm.astype(BF16), kr_b, (((1,), (1,)), ((), ())),
                                preferred_element_type=F32) * decay
            o = jnp.dot(a.astype(BF16), v_b, preferred_element_type=F32)
            state = state_ref[hd]
            q_dec = qm * jnp.exp((pos + 1.0) * log_g)
            o = o + jnp.dot(q_dec.astype(BF16), state.astype(BF16), preferred_element_type=F32)
            k_dec = km * jnp.exp((blk - 1.0 - pos) * log_g)
            kv = lax.dot_general(k_dec.astype(BF16), v_b, (((0,), (0,)), ((), ())),
                                 preferred_element_type=F32)
            state_ref[hd] = state * jnp.exp(blk * log_g) + kv
            mu = jnp.mean(o, axis=-1, keepdims=True)
            var = jnp.mean(jnp.square(o - mu), axis=-1, keepdims=True)
            on = (o - mu) * lax.rsqrt(var + LN_EPS) * gain_ref[:, vs]
            gate = g_ref[0, :, vs]
            o_ref[0, :, vs] = (gate * jax.nn.sigmoid(gate) * on).astype(o_ref.dtype)


def _rope_tables(seq, n_heads):
    half = R_QK_DIM // 2
    freqs = ROPE_BASE ** (-jnp.arange(half, dtype=F32) / half)
    ang = jnp.arange(seq).astype(F32)[:, None] * freqs[None, :]
    cos, sin = jnp.cos(ang), jnp.sin(ang)
    cos_t = jnp.tile(jnp.concatenate([cos, cos], axis=1), (1, n_heads))
    sin_t = jnp.tile(jnp.concatenate([-sin, sin], axis=1), (1, n_heads))
    return cos_t, sin_t


def _retention(proj_rest, ret_gain, *, r_width, blk):
    bsz, seq, _ = proj_rest.shape
    n_heads = r_width // R_V_DIM
    qk_width = n_heads * R_QK_DIM
    cos_t, sin_t = _rope_tables(seq, n_heads)
    kernel = functools.partial(_retention_kernel, blk=blk, n_heads=n_heads)
    return pl.pallas_call(
        kernel,
        grid=(bsz, seq // blk),
        in_specs=[
            pl.BlockSpec((1, blk, qk_width), lambda b, t: (b, t, 0)),
            pl.BlockSpec((1, blk, qk_width), lambda b, t: (b, t, 1)),
            pl.BlockSpec((1, blk, r_width), lambda b, t: (b, t, 1)),
            pl.BlockSpec((1, blk, r_width), lambda b, t: (b, t, 2)),
            pl.BlockSpec((blk, qk_width), lambda b, t: (t, 0)),
            pl.BlockSpec((blk, qk_width), lambda b, t: (t, 0)),
            _resident((1, r_width), lambda b, t: (0, 0)),
        ],
        out_specs=pl.BlockSpec((1, blk, r_width), lambda b, t: (b, t, 0)),
        out_shape=jax.ShapeDtypeStruct((bsz, seq, r_width), BF16),
        scratch_shapes=[pltpu.VMEM((n_heads, LANES, R_V_DIM), F32)],
        compiler_params=_params("arbitrary", "arbitrary"),
        name="retention",
    )(proj_rest, proj_rest, proj_rest, proj_rest, cos_t, sin_t, ret_gain.reshape(1, r_width))


def _largest_tile(n, cap, multiple=LANES):
    best = None
    for t in range(multiple, min(n, cap) + 1, multiple):
        if n % t == 0:
            best = t
    assert best is not None, (n, cap)
    return best


def kernel(x, norm_gains, ffn_w_gate_up, ffn_w_down, ab_w_in, ab_conv_w, ab_pool_w, ab_pool_scale,
           ab_w_out, cd_w_in, ret_norm_gain, cd_w_out, rel_bias_table, final_norm):
    bsz, seq, d = x.shape
    m = bsz * seq
    depth = norm_gains.shape[0]
    a_width = ab_conv_w.shape[-1]
    b_width = ab_pool_scale.shape[-1]
    c_width = cd_w_out.shape[1] // 2
    r_width = ret_norm_gain.shape[-1]
    qk_width = (r_width // R_V_DIM) * R_QK_DIM
    assert c_width == IDX_HEADS * IDX_DIM == r_width and 2 * qk_width == r_width

    tm = _largest_tile(m, 512, 8)
    tf = _largest_tile(ffn_w_down.shape[2], 512)
    seq_tile = _largest_tile(seq, 256, CHUNK)
    pool_rows = _largest_tile(seq, 512, 8)

    w_gu = ffn_w_gate_up.astype(BF16)
    w_dn = ffn_w_down.astype(BF16)

    h = x.reshape(m, d)
    for layer in range(depth):
        j = layer // 2
        h = _ffn(h, norm_gains[layer, 0], w_gu, w_dn, layer, 0, tm=tm, tf=tf)
        if layer % 2 == 0:
            w_in = ab_w_in[j].astype(BF16)
            proj = _norm_matmul(h, norm_gains[layer, 1], w_in, F32,
                                tm=tm, tn=_largest_tile(w_in.shape[1], 2048))
            proj = proj.reshape(bsz, seq, -1)
            y1 = _gated_conv(proj, ab_conv_w[j], cb=2 * LANES, rows=pool_rows)
            y2 = _multiscale_pool(proj, ab_pool_w[j].astype(BF16), ab_pool_scale[j],
                                  col0=3 * a_width, rows=pool_rows)
            w_out = ab_w_out[j].astype(BF16)
        else:
            w_in = cd_w_in[j]
            o_iq = 3 * c_width
            o_ik = o_iq + IDX_HEADS * IDX_DIM
            o_rq = o_ik + IDX_DIM + IDX_HEADS
            n_rest = 2 * qk_width + 2 * r_width
            w_qkv = w_in[:, :o_ik].astype(BF16)
            pad = LANES - (IDX_DIM + IDX_HEADS)
            w_rest = jnp.concatenate(
                [w_in[:, o_rq:o_rq + n_rest], w_in[:, o_ik:o_rq], jnp.zeros((d, pad), F32)],
                axis=1).astype(BF16)
            proj_qkv = _norm_matmul(h, norm_gains[layer, 1], w_qkv, BF16,
                                    tm=tm, tn=_largest_tile(w_qkv.shape[1], 2048))
            proj_rest = _norm_matmul(h, norm_gains[layer, 1], w_rest, F32,
                                     tm=tm, tn=_largest_tile(w_rest.shape[1], 3200))
            y1 = _dsa_attention(proj_qkv.reshape(bsz, seq, -1), proj_rest.reshape(bsz, seq, -1),
                                rel_bias_table, c_width=c_width,
                                rest_idx_block=n_rest // LANES, tq=seq_tile)
            y2 = _retention(proj_rest.reshape(bsz, seq, -1), ret_norm_gain[j],
                            r_width=r_width, blk=seq_tile)
            w_out = cd_w_out[j].astype(BF16)
        h = _out_proj(h, y1.reshape(m, -1), y2.reshape(m, -1), w_out, tm=tm)
        h = _ffn(h, norm_gains[layer, 2], w_gu, w_dn, layer, 1, tm=tm, tf=tf,
                 closing_gain=final_norm if layer == depth - 1 else None)
    return h.reshape(bsz, seq, d)
```

```python
import functools
import math

import numpy as np
import jax
import jax.numpy as jnp
from jax import lax
from jax.experimental import pallas as pl
from jax.experimental.pallas import tpu as pltpu

F32 = jnp.float32
BF16 = jnp.bfloat16
I32 = jnp.int32

LANES = 128
VMEM_LIMIT_BYTES = 56 * 1024 * 1024

CHUNK = 64
RMS_EPS = 1e-6
LN_EPS = 1e-5
NEG_INF = -1e30
CONV_W = 3
POOL_WINDOWS = (2, 4, 8, 16)
POOL_HALO = 16
C_HEAD_DIM = 128
IDX_HEADS = 16
IDX_DIM = 64
TOPK_MAX = 256
NUM_BUCKETS = 32
MAX_DISTANCE = 128
R_QK_DIM = 64
R_V_DIM = 128
ROPE_BASE = 10000.0

INT_MIN = -2 ** 31


def _sortable_key_of(value):
    bits = int(np.array(value, np.float32).view(np.int32))
    return bits if bits >= 0 else bits ^ 0x7FFFFFFF


KEY_NEG_INF = _sortable_key_of(NEG_INF)


def _params(*semantics):
    return pltpu.CompilerParams(dimension_semantics=semantics, vmem_limit_bytes=VMEM_LIMIT_BYTES)


def _resident(block_shape, index_map):
    return pl.BlockSpec(block_shape, index_map, pipeline_mode=pl.Buffered(1))


def _rmsnorm_rows(x, gain):
    y = x * lax.rsqrt(jnp.mean(x * x, axis=-1, keepdims=True) + RMS_EPS)
    return y * gain


def _ffn_kernel(h_ref, g_ref, wg_ref, wu_ref, wd_ref, *rest, n_hidden_tiles, closing_norm):
    if closing_norm:
        fg_ref, o_ref, xn_ref = rest
    else:
        o_ref, xn_ref = rest
    f = pl.program_id(1)

    @pl.when(f == 0)
    def _():
        x = h_ref[...]
        xn_ref[...] = _rmsnorm_rows(x, g_ref[...]).astype(BF16)
        o_ref[...] = x

    xn = xn_ref[...]
    gate = jnp.dot(xn, wg_ref[...], preferred_element_type=F32)
    up = jnp.dot(xn, wu_ref[...], preferred_element_type=F32)
    act = (0.5 * (gate * jax.nn.sigmoid(gate)) * up).astype(BF16)
    o_ref[...] += jnp.dot(act, wd_ref[...], preferred_element_type=F32)

    if closing_norm:
        @pl.when(f == n_hidden_tiles - 1)
        def _():
            o_ref[...] = _rmsnorm_rows(o_ref[...], fg_ref[...])


def _ffn(h, gain, w_gate_up, w_down, layer, which, *, tm, tf, closing_gain=None):
    m, d = h.shape
    d_ff = w_down.shape[2]
    nf = d_ff // tf
    closing_norm = closing_gain is not None
    in_specs = [
        pl.BlockSpec((tm, d), lambda i, f: (i, 0)),
        _resident((1, d), lambda i, f: (0, 0)),
        pl.BlockSpec((None, None, d, tf), lambda i, f: (layer, which, 0, f)),
        pl.BlockSpec((None, None, d, tf), lambda i, f: (layer, which, 0, f + nf)),
        pl.BlockSpec((None, None, tf, d), lambda i, f: (layer, which, f, 0)),
    ]
    args = [h, gain.reshape(1, d), w_gate_up, w_gate_up, w_down]
    if closing_norm:
        in_specs.append(_resident((1, d), lambda i, f: (0, 0)))
        args.append(closing_gain.reshape(1, d))
    return pl.pallas_call(
        functools.partial(_ffn_kernel, n_hidden_tiles=nf, closing_norm=closing_norm),
        grid=(m // tm, nf),
        in_specs=in_specs,
        out_specs=pl.BlockSpec((tm, d), lambda i, f: (i, 0)),
        out_shape=jax.ShapeDtypeStruct((m, d), F32),
        scratch_shapes=[pltpu.VMEM((tm, d), BF16)],
        compiler_params=_params("arbitrary", "arbitrary"),
        name="ffn",
    )(*args)


def _norm_matmul_kernel(h_ref, g_ref, w_ref, o_ref):
    xn = _rmsnorm_rows(h_ref[...], g_ref[...]).astype(BF16)
    o_ref[...] = jnp.dot(xn, w_ref[...], preferred_element_type=F32).astype(o_ref.dtype)


def _norm_matmul(h, gain, w, out_dtype, *, tm, tn):
    m, d = h.shape
    n = w.shape[1]
    return pl.pallas_call(
        _norm_matmul_kernel,
        grid=(n // tn, m // tm),
        in_specs=[
            pl.BlockSpec((tm, d), lambda j, i: (i, 0)),
            _resident((1, d), lambda j, i: (0, 0)),
            pl.BlockSpec((d, tn), lambda j, i: (0, j)),
        ],
        out_specs=pl.BlockSpec((tm, tn), lambda j, i: (i, j)),
        out_shape=jax.ShapeDtypeStruct((m, n), out_dtype),
        compiler_params=_params("arbitrary", "arbitrary"),
        name="norm_matmul",
    )(h, gain.reshape(1, d), w)


def _out_proj_kernel(h_ref, y1_ref, y2_ref, w1_ref, w2_ref, o_ref):
    acc = jnp.dot(y1_ref[...], w1_ref[...], preferred_element_type=F32)
    acc += jnp.dot(y2_ref[...], w2_ref[...], preferred_element_type=F32)
    o_ref[...] = h_ref[...] + acc


def _out_proj(h, y1, y2, w, *, tm):
    m, d = h.shape
    half = y1.shape[1]
    return pl.pallas_call(
        _out_proj_kernel,
        grid=(m // tm,),
        in_specs=[
            pl.BlockSpec((tm, d), lambda i: (i, 0)),
            pl.BlockSpec((tm, half), lambda i: (i, 0)),
            pl.BlockSpec((tm, half), lambda i: (i, 0)),
            _resident((half, d), lambda i: (0, 0)),
            _resident((half, d), lambda i: (1, 0)),
        ],
        out_specs=pl.BlockSpec((tm, d), lambda i: (i, 0)),
        out_shape=jax.ShapeDtypeStruct((m, d), F32),
        compiler_params=_params("arbitrary"),
        name="out_proj",
    )(h, y1, y2, w, w)


def _rows_with_halo(load, r0, rows, width):
    if r0 == 0:
        return jnp.concatenate([jnp.zeros((POOL_HALO, width), F32), load(0, rows)], axis=0)
    return load(r0 - POOL_HALO, rows + POOL_HALO)


def _conv_kernel(b_ref, c_ref, v_ref, w_ref, o_ref, *, rows):
    t, cb = b_ref.shape[1], b_ref.shape[2]
    w0, w1, w2 = w_ref[0:1, :], w_ref[1:2, :], w_ref[2:3, :]
    for r0 in range(0, t, rows):
        ext = _rows_with_halo(lambda s, n: c_ref[0, s:s + n, :] * v_ref[0, s:s + n, :], r0, rows, cb)
        u0 = ext[POOL_HALO:]
        u1 = pltpu.roll(ext, 1, axis=0)[POOL_HALO:]
        u2 = pltpu.roll(ext, 2, axis=0)[POOL_HALO:]
        y = w0 * u2 + w1 * u1 + w2 * u0
        o_ref[0, r0:r0 + rows, :] = (b_ref[0, r0:r0 + rows, :] * y).astype(o_ref.dtype)


def _gated_conv(proj, conv_w, *, cb, rows):
    bsz, t, _ = proj.shape
    a_width = conv_w.shape[1]
    ncb = a_width // cb
    return pl.pallas_call(
        functools.partial(_conv_kernel, rows=rows),
        grid=(bsz, ncb),
        in_specs=[
            pl.BlockSpec((1, t, cb), lambda b, c: (b, 0, c)),
            pl.BlockSpec((1, t, cb), lambda b, c: (b, 0, c + ncb)),
            pl.BlockSpec((1, t, cb), lambda b, c: (b, 0, c + 2 * ncb)),
            pl.BlockSpec((CONV_W, cb), lambda b, c: (0, c)),
        ],
        out_specs=pl.BlockSpec((1, t, cb), lambda b, c: (b, 0, c)),
        out_shape=jax.ShapeDtypeStruct((bsz, t, a_width), BF16),
        compiler_params=_params("arbitrary", "arbitrary"),
        name="gated_conv",
    )(proj, proj, proj, conv_w)


def _pool_kernel(x_ref, w_ref, s_ref, o_ref, *, rows):
    t, gw = x_ref.shape[1], x_ref.shape[2]
    group = pl.program_id(1)
    wmat = w_ref[0]
    scale = s_ref[...]
    for gi, window in enumerate(POOL_WINDOWS):
        @pl.when(group == gi)
        def _(window=window):
            for r0 in range(0, t, rows):
                ext = _rows_with_halo(lambda s, n: x_ref[0, s:s + n, :], r0, rows, gw)
                acc = ext
                span = 1
                while span < window:
                    acc = acc + pltpu.roll(acc, span, axis=0)
                    span *= 2
                pos = r0 + lax.broadcasted_iota(I32, (rows, 1), 0)
                count = jnp.minimum(pos + 1, window).astype(F32)
                pooled = acc[POOL_HALO:] / count - ext[POOL_HALO:]
                mixed = jnp.dot(pooled.astype(BF16), wmat, preferred_element_type=F32)
                o_ref[0, r0:r0 + rows, :] = (mixed * scale).astype(o_ref.dtype)


def _multiscale_pool(proj, pool_w, pool_scale, *, col0, rows):
    bsz, t, _ = proj.shape
    ngroup, gw, _ = pool_w.shape
    cblk0 = col0 // gw
    return pl.pallas_call(
        functools.partial(_pool_kernel, rows=rows),
        grid=(bsz, ngroup),
        in_specs=[
            pl.BlockSpec((1, t, gw), lambda b, g: (b, 0, cblk0 + g)),
            pl.BlockSpec((1, gw, gw), lambda b, g: (g, 0, 0)),
            pl.BlockSpec((1, gw), lambda b, g: (0, g)),
        ],
        out_specs=pl.BlockSpec((1, t, gw), lambda b, g: (b, 0, g)),
        out_shape=jax.ShapeDtypeStruct((bsz, t, ngroup * gw), BF16),
        compiler_params=_params("arbitrary", "arbitrary"),
        name="multiscale_pool",
    )(proj, pool_w, pool_scale.reshape(1, ngroup * gw))


_NT = (((1,), (1,)), ((), ()))


def _dsa_kernel(q_ref, k_ref, vt_ref, iq_ref, ikw_ref, iwt_ref, bias_ref, o_ref,
                kk_ref, keys_ref, cut_ref, mask_ref, logit_ref, acc_ref,
                *, seq, tq, topk, n_heads):
    qt = pl.program_id(1)
    kt_w = tq
    n_pairs = (qt + 2) // 2
    n_tiles = 2 * n_pairs
    n_unseen = seq - n_tiles * kt_w

    @pl.when(qt == 0)
    def _():
        for j in range(seq // kt_w):
            x = ikw_ref[0, j * kt_w:(j + 1) * kt_w, :]
            lane_t = lax.broadcasted_iota(I32, x.shape, 1)
            kk_ref[j, 0:kt_w, :] = jnp.where(lane_t < IDX_DIM, x, 0.0).astype(BF16)
            kk_ref[j, kt_w:2 * kt_w, :] = jnp.where(
                lane_t >= IDX_DIM, pltpu.roll(x, IDX_DIM, axis=1), 0.0).astype(BF16)

    w_all = iwt_ref[0] * ((IDX_HEADS ** -0.5) * (IDX_DIM ** -0.5))
    q_pos = qt * tq + lax.broadcasted_iota(I32, (1, tq), 1)
    vis_end = (q_pos // CHUNK + 1) * CHUNK
    key_row = lax.broadcasted_iota(I32, (kt_w, tq), 0)

    def tile_start(j):
        return pl.multiple_of(j * kt_w, kt_w)

    def score_tile(j, carry):
        s0 = tile_start(j)
        kk = kk_ref[j]
        score = jnp.zeros((kt_w, tq), F32)
        for pair in range(IDX_HEADS // 2):
            a = iq_ref[0, :, pair * LANES:(pair + 1) * LANES]
            d = lax.dot_general(kk, a, _NT, preferred_element_type=F32)
            score = score + w_all[2 * pair:2 * pair + 1] * jnp.maximum(d[:kt_w], 0.0)
            score = score + w_all[2 * pair + 1:2 * pair + 2] * jnp.maximum(d[kt_w:], 0.0)
        bits = pltpu.bitcast(score, I32)
        key = jnp.where(bits >= 0, bits, bits ^ 0x7FFFFFFF)
        key = jnp.where(s0 + key_row < vis_end, key, KEY_NEG_INF)
        keys_ref[pl.ds(s0, kt_w), :] = key
        return carry

    lax.fori_loop(0, n_tiles, score_tile, 0)

    def count_tiles(pred):
        def body(jj, accs):
            new = []
            for u in range(2):
                s0 = tile_start(2 * jj + u)
                kt = keys_ref[pl.ds(s0, kt_w), :]
                hit = jnp.where(pred(kt, s0 + key_row), 1.0, 0.0)
                new.append(accs[u] + jnp.sum(hit.reshape(kt_w // 8, 8, tq), axis=0))
            return tuple(new)
        zero = jnp.zeros((8, tq), F32)
        acc0, acc1 = lax.fori_loop(0, n_pairs, body, (zero, zero))
        return jnp.sum(acc0 + acc1, axis=0, keepdims=True)

    def unseen_if(cond):
        return jnp.where(cond, float(1.0) * n_unseen, 0.0)

    def bit_step(i, thr):
        cand = thr + lax.shift_left(jnp.int32(1), 31 - i)
        cnt = count_tiles(lambda kt, idx: kt >= cand) + unseen_if(cand <= KEY_NEG_INF)
        return jnp.where(cnt >= topk, cand, thr)

    thr = lax.fori_loop(0, 32, bit_step, jnp.full((1, tq), INT_MIN, I32))

    cnt_ge = count_tiles(lambda kt, idx: kt >= thr) + unseen_if(thr <= KEY_NEG_INF)
    cnt_gt = count_tiles(lambda kt, idx: kt > thr) + unseen_if(thr < KEY_NEG_INF)
    need = topk - cnt_gt
    idx_bits = int(seq).bit_length()
    cut_ref[...] = jnp.full((1, tq), 2 ** idx_bits - 1, I32)
    has_tie = jnp.max(jnp.where(cnt_ge != topk, 1.0, 0.0)) > 0.0

    @pl.when(has_tie)
    def _():
        def idx_step(i, cut):
            cand = cut + lax.shift_left(jnp.int32(1), idx_bits - 1 - i)
            below = count_tiles(lambda kt, idx: (kt == thr) & (idx < cand))
            return jnp.where(below <= need, cand, cut)
        cut_ref[...] = lax.fori_loop(0, idx_bits, idx_step, jnp.zeros((1, tq), I32))

    cut = cut_ref[...]

    def mask_tile(j, carry):
        s0 = tile_start(j)
        kt = keys_ref[pl.ds(s0, kt_w), :]
        s_idx = s0 + key_row
        sel = ((kt > thr) | ((kt == thr) & (s_idx < cut))) & (s_idx < vis_end)
        mask_ref[pl.ds(s0, kt_w), :] = jnp.where(sel, 0.0, NEG_INF)
        return carry

    lax.fori_loop(0, n_tiles, mask_tile, 0)

    scale = C_HEAD_DIM ** -0.5

    def fold8(x, op):
        return op(x.reshape(kt_w // 8, 8, tq), axis=0)

    def head_pair_body(hp, carry):
        h0s = [pl.multiple_of((2 * hp + e) * C_HEAD_DIM, C_HEAD_DIM) for e in range(2)]

        def logits_pair(jj, mxs):
            new = []
            for e in range(2):
                qh = q_ref[0, :, pl.ds(h0s[e], C_HEAD_DIM)]
                mx = mxs[e]
                for u in range(2):
                    j = 2 * jj + u
                    s0 = tile_start(j)
                    x0 = pl.multiple_of(jnp.clip(j - qt + 2, 0, 2) * kt_w, kt_w)
                    lg = lax.dot_general(k_ref[0, pl.ds(s0, kt_w), pl.ds(h0s[e], C_HEAD_DIM)], qh, _NT,
                                         preferred_element_type=F32)
                    lg = lg * scale + bias_ref[2 * hp + e, pl.ds(x0, kt_w), :] + mask_ref[pl.ds(s0, kt_w), :]
                    logit_ref[e, pl.ds(s0, kt_w), :] = lg
                    mx = jnp.maximum(mx, fold8(lg, jnp.max))
                new.append(mx)
            return tuple(new)

        neg = jnp.full((8, tq), NEG_INF, F32)
        mxs = lax.fori_loop(0, n_pairs, logits_pair, (neg, neg))
        ms = [jnp.max(mx, axis=0, keepdims=True) for mx in mxs]
        acc_ref[...] = jnp.zeros(acc_ref.shape, F32)

        def weigh_pair(jj, ls):
            s0 = pl.multiple_of(jj * 2 * kt_w, 2 * kt_w)
            new = []
            for e in range(2):
                p = jnp.exp(logit_ref[e, pl.ds(s0, 2 * kt_w), :] - ms[e])
                acc_ref[e] += jnp.dot(vt_ref[0, pl.ds(h0s[e], C_HEAD_DIM), pl.ds(s0, 2 * kt_w)],
                                      p.astype(BF16), preferred_element_type=F32)
                new.append(ls[e] + jnp.sum(p.reshape(2 * kt_w // 8, 8, tq), axis=0))
            return tuple(new)

        zero = jnp.zeros((8, tq), F32)
        ls = lax.fori_loop(0, n_pairs, weigh_pair, (zero, zero))
        for e in range(2):
            out = acc_ref[e] / jnp.sum(ls[e], axis=0, keepdims=True)
            o_ref[0, :, pl.ds(h0s[e], C_HEAD_DIM)] = out.T.astype(o_ref.dtype)
        return carry

    lax.fori_loop(0, n_heads // 2, head_pair_body, 0)


def _t5_bucket(rel):
    nb = NUM_BUCKETS // 2
    max_exact = nb // 2
    ret = jnp.where(rel > 0, nb, 0)
    n = jnp.abs(rel)
    large = max_exact + (jnp.log(jnp.maximum(n, 1).astype(F32) / max_exact)
                         / math.log(MAX_DISTANCE / max_exact) * (nb - max_exact)).astype(I32)
    large = jnp.minimum(large, nb - 1)
    return ret + jnp.where(n < max_exact, n, large)


def _bias_slab(rel_bias_table, tq):
    assert tq >= MAX_DISTANCE
    rel = jnp.arange(3 * tq, dtype=I32)[:, None] - 2 * tq - jnp.arange(tq, dtype=I32)[None, :]
    bucket = _t5_bucket(rel)[None]
    table = rel_bias_table.astype(F32)
    slab = jnp.zeros((table.shape[1], 3 * tq, tq), F32)
    for b in range(NUM_BUCKETS):
        slab = jnp.where(bucket == b, table[b][:, None, None], slab)
    return slab


def _dsa_attention(proj_qkv, proj_rest, rel_bias_table, *, c_width, rest_idx_block, tq):
    bsz, seq, _ = proj_qkv.shape
    n_heads = c_width // C_HEAD_DIM
    topk = min(TOPK_MAX, seq // 4)
    assert (seq // tq) % 2 == 0 and n_heads % 2 == 0
    slab = _bias_slab(rel_bias_table, tq)
    v_t = jnp.swapaxes(proj_qkv[..., 2 * c_width:3 * c_width], 1, 2)
    w0 = rest_idx_block * LANES + IDX_DIM
    iw_t = jnp.swapaxes(proj_rest[..., w0:w0 + IDX_HEADS], 1, 2)
    kernel = functools.partial(_dsa_kernel, seq=seq, tq=tq, topk=topk, n_heads=n_heads)
    return pl.pallas_call(
        kernel,
        grid=(bsz, seq // tq),
        in_specs=[
            pl.BlockSpec((1, tq, c_width), lambda b, t: (b, t, 0)),
            pl.BlockSpec((1, seq, c_width), lambda b, t: (b, 0, 1), pipeline_mode=pl.Buffered(1)),
            pl.BlockSpec((1, c_width, seq), lambda b, t: (b, 0, 0), pipeline_mode=pl.Buffered(1)),
            pl.BlockSpec((1, tq, IDX_HEADS * IDX_DIM), lambda b, t: (b, t, 3)),
            pl.BlockSpec((1, seq, LANES), lambda b, t: (b, 0, rest_idx_block), pipeline_mode=pl.Buffered(1)),
            pl.BlockSpec((1, IDX_HEADS, tq), lambda b, t: (b, 0, t)),
            _resident((n_heads, 3 * tq, tq), lambda b, t: (0, 0, 0)),
        ],
        out_specs=pl.BlockSpec((1, tq, c_width), lambda b, t: (b, t, 0)),
        out_shape=jax.ShapeDtypeStruct((bsz, seq, c_width), BF16),
        scratch_shapes=[
            pltpu.VMEM((seq // tq, 2 * tq, LANES), BF16),
            pltpu.VMEM((seq, tq), I32),
            pltpu.VMEM((1, tq), I32),
            pltpu.VMEM((seq, tq), F32),
            pltpu.VMEM((2, seq, tq), F32),
            pltpu.VMEM((2, C_HEAD_DIM, tq), F32),
        ],
        compiler_params=_params("arbitrary", "arbitrary"),
        name="dsa_attention",
    )(proj_qkv, proj_qkv, v_t, proj_qkv, proj_rest, iw_t, slab)


def _rope_pairs(x, cos, sin_signed):
    lane = lax.broadcasted_iota(I32, x.shape, 1)
    half = R_QK_DIM // 2
    swapped = jnp.where((lane % R_QK_DIM) < half,
                        pltpu.roll(x, LANES - half, axis=1), pltpu.roll(x, half, axis=1))
    return x * cos + swapped * sin_signed


def _retention_kernel(q_ref, k_ref, v_ref, g_ref, cos_ref, sin_ref, gain_ref, o_ref,
                      state_ref, *, blk, n_heads):
    tb = pl.program_id(1)

    @pl.when(tb == 0)
    def _():
        state_ref[...] = jnp.zeros(state_ref.shape, F32)

    lane = lax.broadcasted_iota(I32, (blk, LANES), 1)
    ri = lax.broadcasted_iota(I32, (blk, blk), 0)
    ci = lax.broadcasted_iota(I32, (blk, blk), 1)
    visible = (ci // CHUNK) <= (ri // CHUNK)
    dist = jnp.abs(ri - ci).astype(F32)
    pos = lax.broadcasted_iota(I32, (blk, 1), 0).astype(F32)

    for pair in range(n_heads // 2):
        ls = slice(pair * LANES, (pair + 1) * LANES)
        qr = _rope_pairs(q_ref[0, :, ls], cos_ref[:, ls], sin_ref[:, ls])
        kr = _rope_pairs(k_ref[0, :, ls], cos_ref[:, ls], sin_ref[:, ls]) * (R_QK_DIM ** -0.5)
        kr_b = kr.astype(BF16)
        for e in range(2):
            hd = 2 * pair + e
            vs = slice(hd * R_V_DIM, (hd + 1) * R_V_DIM)
            log_g = jnp.log(jnp.full((1, 1), 1.0 - 2.0 ** (-5.0 - hd), F32))
            mine = (lane >= e * R_QK_DIM) & (lane < (e + 1) * R_QK_DIM)
            qm = jnp.where(mine, qr, 0.0)
            km = jnp.where(mine, kr, 0.0)
            v_b = v_ref[0, :, vs].astype(BF16)
            decay = jnp.where(visible, jnp.exp(dist * log_g), 0.0)
            a = lax.dot_general(qm.astype(BF16), kr_b, (((1,), (1,)), ((), ())),
                                preferred_element_type=F32) * decay
            o = jnp.dot(a.astype(BF16), v_b, preferred_element_type=F32)
            state = state_ref[hd]
            q_dec = qm * jnp.exp((pos + 1.0) * log_g)
            o = o + jnp.dot(q_dec.astype(BF16), state.astype(BF16), preferred_element_type=F32)
            k_dec = km * jnp.exp((blk - 1.0 - pos) * log_g)
            kv = lax.dot_general(k_dec.astype(BF16), v_b, (((0,), (0,)), ((), ())),
                                 preferred_element_type=F32)
            state_ref[hd] = state * jnp.exp(blk * log_g) + kv
            mu = jnp.mean(o, axis=-1, keepdims=True)
            var = jnp.mean(jnp.square(o - mu), axis=-1, keepdims=True)
            on = (o - mu) * lax.rsqrt(var + LN_EPS) * gain_ref[:, vs]
            gate = g_ref[0, :, vs]
            o_ref[0, :, vs] = (gate * jax.nn.sigmoid(gate) * on).astype(o_ref.dtype)


def _rope_tables(seq, n_heads):
    half = R_QK_DIM // 2
    freqs = ROPE_BASE ** (-jnp.arange(half, dtype=F32) / half)
    ang = jnp.arange(seq).astype(F32)[:, None] * freqs[None, :]
    cos, sin = jnp.cos(ang), jnp.sin(ang)
    cos_t = jnp.tile(jnp.concatenate([cos, cos], axis=1), (1, n_heads))
    sin_t = jnp.tile(jnp.concatenate([-sin, sin], axis=1), (1, n_heads))
    return cos_t, sin_t


def _retention(proj_rest, ret_gain, *, r_width, blk):
    bsz, seq, _ = proj_rest.shape
    n_heads = r_width // R_V_DIM
    qk_width = n_heads * R_QK_DIM
    cos_t, sin_t = _rope_tables(seq, n_heads)
    kernel = functools.partial(_retention_kernel, blk=blk, n_heads=n_heads)
    return pl.pallas_call(
        kernel,
        grid=(bsz, seq // blk),
        in_specs=[
            pl.BlockSpec((1, blk, qk_width), lambda b, t: (b, t, 0)),
            pl.BlockSpec((1, blk, qk_width), lambda b, t: (b, t, 1)),
            pl.BlockSpec((1, blk, r_width), lambda b, t: (b, t, 1)),
            pl.BlockSpec((1, blk, r_width), lambda b, t: (b, t, 2)),
            pl.BlockSpec((blk, qk_width), lambda b, t: (t, 0)),
            pl.BlockSpec((blk, qk_width), lambda b, t: (t, 0)),
            _resident((1, r_width), lambda b, t: (0, 0)),
        ],
        out_specs=pl.BlockSpec((1, blk, r_width), lambda b, t: (b, t, 0)),
        out_shape=jax.ShapeDtypeStruct((bsz, seq, r_width), BF16),
        scratch_shapes=[pltpu.VMEM((n_heads, LANES, R_V_DIM), F32)],
        compiler_params=_params("arbitrary", "arbitrary"),
        name="retention",
    )(proj_rest, proj_rest, proj_rest, proj_rest, cos_t, sin_t, ret_gain.reshape(1, r_width))


def _largest_tile(n, cap, multiple=LANES):
    best = None
    for t in range(multiple, min(n, cap) + 1, multiple):
        if n % t == 0:
            best = t
    assert best is not None, (n, cap)
    return best


def kernel(x, norm_gains, ffn_w_gate_up, ffn_w_down, ab_w_in, ab_conv_w, ab_pool_w, ab_pool_scale,
           ab_w_out, cd_w_in, ret_norm_gain, cd_w_out, rel_bias_table, final_norm):
    bsz, seq, d = x.shape
    m = bsz * seq
    depth = norm_gains.shape[0]
    a_width = ab_conv_w.shape[-1]
    b_width = ab_pool_scale.shape[-1]
    c_width = cd_w_out.shape[1] // 2
    r_width = ret_norm_gain.shape[-1]
    qk_width = (r_width // R_V_DIM) * R_QK_DIM
    assert c_width == IDX_HEADS * IDX_DIM == r_width and 2 * qk_width == r_width

    tm = _largest_tile(m, 512, 8)
    tm_ffn = _largest_tile(m, 1024, 8)
    tf = _largest_tile(ffn_w_down.shape[2], 512)
    seq_tile = _largest_tile(seq, 256, CHUNK)
    pool_rows = _largest_tile(seq, 512, 8)

    w_gu = ffn_w_gate_up.astype(BF16)
    w_dn = ffn_w_down.astype(BF16)

    h = x.reshape(m, d)
    for layer in range(depth):
        j = layer // 2
        h = _ffn(h, norm_gains[layer, 0], w_gu, w_dn, layer, 0, tm=tm_ffn, tf=tf)
        if layer % 2 == 0:
            w_in = ab_w_in[j].astype(BF16)
            proj = _norm_matmul(h, norm_gains[layer, 1], w_in, F32,
                                tm=tm, tn=_largest_tile(w_in.shape[1], 2048))
            proj = proj.reshape(bsz, seq, -1)
            y1 = _gated_conv(proj, ab_conv_w[j], cb=2 * LANES, rows=pool_rows)
            y2 = _multiscale_pool(proj, ab_pool_w[j].astype(BF16), ab_pool_scale[j],
                                  col0=3 * a_width, rows=pool_rows)
            w_out = ab_w_out[j].astype(BF16)
        else:
            w_in = cd_w_in[j]
            o_iq = 3 * c_width
            o_ik = o_iq + IDX_HEADS * IDX_DIM
            o_rq = o_ik + IDX_DIM + IDX_HEADS
            n_rest = 2 * qk_width + 2 * r_width
            w_qkv = w_in[:, :o_ik].astype(BF16)
            pad = LANES - (IDX_DIM + IDX_HEADS)
            w_rest = jnp.concatenate(
                [w_in[:, o_rq:o_rq + n_rest], w_in[:, o_ik:o_rq], jnp.zeros((d, pad), F32)],
                axis=1).astype(BF16)
            proj_qkv = _norm_matmul(h, norm_gains[layer, 1], w_qkv, BF16,
                                    tm=tm, tn=_largest_tile(w_qkv.shape[1], 2048))
            proj_rest = _norm_matmul(h, norm_gains[layer, 1], w_rest, F32,
                                     tm=tm, tn=_largest_tile(w_rest.shape[1], 3200))
            y1 = _dsa_attention(proj_qkv.reshape(bsz, seq, -1), proj_rest.reshape(bsz, seq, -1),
                                rel_bias_table, c_width=c_width,
                                rest_idx_block=n_rest // LANES, tq=seq_tile)
            y2 = _retention(proj_rest.reshape(bsz, seq, -1), ret_norm_gain[j],
                            r_width=r_width, blk=seq_tile)
            w_out = cd_w_out[j].astype(BF16)
        h = _out_proj(h, y1.reshape(m, -1), y2.reshape(m, -1), w_out, tm=tm)
        h = _ffn(h, norm_gains[layer, 2], w_gu, w_dn, layer, 1, tm=tm_ffn, tf=tf,
                 closing_gain=final_norm if layer == depth - 1 else None)
    return h.reshape(bsz, seq, d)
```
